```python
import math
import jax, jax.numpy as jnp
from jax import lax
import numpy as np

D_MODEL = 1024
BATCH = 4
SEQ = 4096
DEPTH = 4
DEC_BATCH = 128
DEC_SEQ = 8
PAST_LEN = 8192
PAGE_SIZE = 128

N_MIXERS = 3
N_A = (DEPTH + 2) // 3
N_B = (DEPTH + 1) // 3
N_C = DEPTH // 3
ROPE_THETA = 10000.0
NORM_EPS = 1e-6
BLOCK_Q = 128

A_GROUPS = ((128, 1), (512, 4), (2048, 16))
A_HEADS = 8
A_HEAD_DIM = 64
A_SCALE = A_HEAD_DIM ** -0.5

B_HEADS = 8
B_KV_HEADS = 2
B_HEAD_DIM = 64
B_SCALE = B_HEAD_DIM ** -0.5

C_HEADS = 8
C_Q_LORA = 384
C_KV_LORA = 256
C_NOPE = 128
C_ROPE = 64
C_V = 128
C_SCALE = (C_NOPE + C_ROPE) ** -0.5

D_FF = 2816
CONV_W = 3

kernel_name = 'hybrid_dilated_diff_mla_convffn_step'


def rms_norm(x, g):
    xf = x.astype(jnp.float32)
    y = xf * lax.rsqrt(jnp.mean(xf * xf, axis=-1, keepdims=True) + NORM_EPS)
    return (y * g.astype(jnp.float32)).astype(x.dtype)


def rope(x, pos):
    half = x.shape[-1] // 2
    inv = ROPE_THETA ** (-jnp.arange(half, dtype=jnp.float32) / half)
    ang = pos.astype(jnp.float32)[:, None] * inv[None, :]
    shape = (1, pos.shape[0]) + (1,) * (x.ndim - 3) + (half,)
    cos, sin = jnp.cos(ang).reshape(shape), jnp.sin(ang).reshape(shape)
    xf = x.astype(jnp.float32)
    x1, x2 = xf[..., :half], xf[..., half:]
    return jnp.concatenate([x1 * cos - x2 * sin, x2 * cos + x1 * sin], axis=-1).astype(x.dtype)


def softmax_stats(s):
    m = jnp.max(s, axis=-1, keepdims=True)
    e = jnp.exp(s - m)
    den = jnp.sum(e, axis=-1, keepdims=True)
    return e / den, (m + jnp.log(den))[..., 0]


def causal_query_blocks(attend, q):
    b, s = q.shape[:2]
    nb = s // BLOCK_Q
    qb = jnp.moveaxis(q.reshape((b, nb, BLOCK_Q) + q.shape[2:]), 1, 0)
    out = lax.map(lambda a: attend(a[0], a[1] * BLOCK_Q + jnp.arange(BLOCK_Q)), (qb, jnp.arange(nb)))
    out = jnp.moveaxis(out, 0, 1)
    return out.reshape((b, s) + out.shape[3:])


def dilated_project(h, w_qkv, q_norm, k_norm, pos):
    b, s = h.shape[:2]
    qkv = (h @ w_qkv).reshape(b, s, len(A_GROUPS), 3, A_HEADS, A_HEAD_DIM)
    q = rope(rms_norm(qkv[:, :, :, 0], q_norm[:, None, :]), pos)
    k = rope(rms_norm(qkv[:, :, :, 1], k_norm[:, None, :]), pos)
    return q, k, qkv[:, :, :, 2]


def dilated_prompt(q, k, v, window, dil):
    b, s, h, dh = q.shape
    n = window // dil
    span = dil * BLOCK_Q
    s_pad = -(-s // span) * span
    m_len = s_pad // dil
    nb = m_len // BLOCK_Q

    def split(x):
        x = jnp.pad(x, ((0, 0), (0, s_pad - s), (0, 0), (0, 0)))
        x = x.reshape(b, m_len, dil, h, dh).transpose(0, 2, 1, 3, 4)
        return x.reshape(b, dil, nb, BLOCK_Q, h, dh)

    def with_prev(x):
        prev = jnp.pad(x[:, :, :-1], ((0, 0), (0, 0), (1, 0), (0, 0), (0, 0), (0, 0)))
        return jnp.concatenate([prev, x], axis=3)

    qs, kb, vb = split(q), with_prev(split(k)), with_prev(split(v))
    qi = jnp.arange(BLOCK_Q)[:, None] + BLOCK_Q
    ki = jnp.arange(2 * BLOCK_Q)[None, :]
    band = (qi - ki >= 0) & (qi - ki <= n)
    mask = band[None] & ((jnp.arange(nb) > 0)[:, None, None] | (ki >= BLOCK_Q)[None])
    sc = jnp.einsum('brnqhd,brnkhd->brnhqk', qs, kb, preferred_element_type=jnp.float32) * A_SCALE
    sc = jnp.where(mask[None, None, :, None], sc, -jnp.inf)
    p, lse = softmax_stats(sc)
    o = jnp.einsum('brnhqk,brnkhd->brnqhd', p, vb.astype(jnp.float32))
    o = o.reshape(b, dil, m_len, h, dh).transpose(0, 2, 1, 3, 4).reshape(b, s_pad, h, dh)[:, :s]
    lse = lse.transpose(0, 1, 2, 4, 3).reshape(b, dil, m_len, h).transpose(0, 2, 1, 3).reshape(b, s_pad, h)[:, :s]
    return o, lse


def dilated_sample(q, k_ext, v_ext, window, dil, buf_len):
    t = q.shape[1]
    n = window // dil
    idx = buf_len + jnp.arange(t)[:, None] - dil * jnp.arange(n + 1)[None, :]
    valid = idx >= 0
    idx = jnp.maximum(idx, 0)
    kg, vg = k_ext[:, idx], v_ext[:, idx]
    sc = jnp.einsum('bqhd,bqkhd->bhqk', q, kg, preferred_element_type=jnp.float32) * A_SCALE
    sc = jnp.where(valid[None, None], sc, -jnp.inf)
    p, lse = softmax_stats(sc)
    o = jnp.einsum('bhqk,bqkhd->bqhd', p, vg.astype(jnp.float32))
    return o, lse.transpose(0, 2, 1)


def merge_groups(outs, lses):
    w = jax.nn.softmax(jnp.stack(lses), axis=0)
    return jnp.sum(w[..., None] * jnp.stack(outs), axis=0)


def dilated_mixer(hp, hs, pos_p, pos_s, bufs, w_qkv, q_norm, k_norm, w_o):
    qp, kp, vp = dilated_project(hp, w_qkv, q_norm, k_norm, pos_p)
    qs, ks, vs = dilated_project(hs, w_qkv, q_norm, k_norm, pos_s)
    s = hp.shape[1]
    op, lp, osm, lsm, rows_p, rows_s = [], [], [], [], [], []
    for g, (window, dil) in enumerate(A_GROUPS):
        o, l = dilated_prompt(qp[:, :, g], kp[:, :, g], vp[:, :, g], window, dil)
        op.append(o)
        lp.append(l)
        buf = bufs[g]
        k_ext = jnp.concatenate([buf[:, :, 0], ks[:, :, g]], axis=1)
        v_ext = jnp.concatenate([buf[:, :, 1], vs[:, :, g]], axis=1)
        o, l = dilated_sample(qs[:, :, g], k_ext, v_ext, window, dil, buf.shape[1])
        osm.append(o)
        lsm.append(l)
        keep = min(window, s)
        rows_p.append(jnp.stack([kp[:, s - keep:, g], vp[:, s - keep:, g]], axis=2))
        rows_s.append(jnp.stack([ks[:, :, g], vs[:, :, g]], axis=2))
    yp = merge_groups(op, lp).astype(hp.dtype).reshape(hp.shape[0], s, -1) @ w_o
    ys = merge_groups(osm, lsm).astype(hs.dtype).reshape(hs.shape[0], hs.shape[1], -1) @ w_o
    return yp, ys, rows_p, rows_s


def diff_project(h, w_qkv, q_norm, k_norm, pos):
    b, s = h.shape[:2]
    nq = B_HEADS * 2 * B_HEAD_DIM
    nk = B_KV_HEADS * 2 * B_HEAD_DIM
    qkv = h @ w_qkv
    q = rope(rms_norm(qkv[..., :nq].reshape(b, s, B_HEADS, 2, B_HEAD_DIM), q_norm), pos)
    k = rope(rms_norm(qkv[..., nq:nq + nk].reshape(b, s, B_KV_HEADS, 2, B_HEAD_DIM), k_norm), pos)
    v = qkv[..., nq + nk:].reshape(b, s, B_KV_HEADS, 2 * B_HEAD_DIM)
    rows = jnp.stack([k.reshape(b, s, B_KV_HEADS, 2 * B_HEAD_DIM), v], axis=2)
    return q, rows


def diff_attend(q, rows, mask, lam):
    b, tq = q.shape[:2]
    tk = rows.shape[1]
    k = rows[:, :, 0].reshape(b, tk, B_KV_HEADS, 2, B_HEAD_DIM)
    v = rows[:, :, 1].astype(jnp.float32)
    qg = q.reshape(b, tq, B_KV_HEADS, B_HEADS // B_KV_HEADS, 2, B_HEAD_DIM)
    sc = jnp.einsum('bqngcd,bkncd->bngcqk', qg, k, preferred_element_type=jnp.float32) * B_SCALE
    p = jax.nn.softmax(jnp.where(mask, sc, -jnp.inf), axis=-1)
    pd = p[:, :, :, 0] - lam * p[:, :, :, 1]
    o = jnp.einsum('bngqk,bknd->bqngd', pd, v)
    return o.reshape(b, tq, B_HEADS, 2 * B_HEAD_DIM)


def diff_mixer(hp, hs, pos_p, pos_s, cache, page_table, layer_idx, w_qkv, q_norm, k_norm, lam_p, subln, w_o):
    lam_init = 0.8 - 0.6 * math.exp(-0.3 * layer_idx)
    lpf = lam_p.astype(jnp.float32)
    lam = jnp.exp(jnp.sum(lpf[0] * lpf[1])) - jnp.exp(jnp.sum(lpf[2] * lpf[3])) + lam_init
    qp, rows_p = diff_project(hp, w_qkv, q_norm, k_norm, pos_p)
    qs, rows_s = diff_project(hs, w_qkv, q_norm, k_norm, pos_s)
    op = causal_query_blocks(lambda qb, qpos: diff_attend(qb, rows_p, pos_p[None, :] <= qpos[:, None], lam), qp)
    past = page_table.shape[1] * PAGE_SIZE
    smask = jnp.arange(past + hs.shape[1])[None, :] <= pos_s[:, None]

    def one_seq(a):
        q_b, r_b, pt = a
        r = jnp.concatenate([cache[pt].reshape((past,) + cache.shape[2:]), r_b], axis=0)
        return diff_attend(q_b[None], r[None], smask, lam)[0]

    osm = lax.map(one_seq, (qs, rows_s, page_table))

    def out(o, h):
        o = rms_norm(o, subln) * (1.0 - lam_init)
        return o.astype(h.dtype).reshape(h.shape[0], h.shape[1], -1) @ w_o

    return out(op, hp), out(osm, hs), rows_p, rows_s


def mla_project(h, w_down, q_lora_norm, w_uq, kv_norm, q_norm, k_norm, pos):
    b, s = h.shape[:2]
    d = h @ w_down
    cq = rms_norm(d[..., :C_Q_LORA], q_lora_norm)
    ckv = rms_norm(d[..., C_Q_LORA:C_Q_LORA + C_KV_LORA], kv_norm)
    kpe = rope(rms_norm(d[..., C_Q_LORA + C_KV_LORA:], k_norm[C_NOPE:])[:, :, None, :], pos)[:, :, 0]
    q = (cq @ w_uq).reshape(b, s, C_HEADS, C_NOPE + C_ROPE)
    q = jnp.concatenate([rms_norm(q[..., :C_NOPE], q_norm[:C_NOPE]),
                         rope(rms_norm(q[..., C_NOPE:], q_norm[C_NOPE:]), pos)], axis=-1)
    return q, jnp.concatenate([ckv, kpe], axis=-1)


def mla_keys(latent, w_ukv, k_norm):
    w_uk = w_ukv.reshape(C_KV_LORA, C_HEADS, C_NOPE + C_V)[..., :C_NOPE]
    return rms_norm(jnp.einsum('bkc,chd->bkhd', latent[..., :C_KV_LORA], w_uk), k_norm[:C_NOPE])


def mla_attend(q, k_nope, latent, w_ukv, mask):
    w_uv = w_ukv.reshape(C_KV_LORA, C_HEADS, C_NOPE + C_V)[..., C_NOPE:]
    sc = (jnp.einsum('bqhd,bkhd->bhqk', q[..., :C_NOPE], k_nope, preferred_element_type=jnp.float32)
          + jnp.einsum('bqhd,bkd->bhqk', q[..., C_NOPE:], latent[..., C_KV_LORA:],
                       preferred_element_type=jnp.float32)) * C_SCALE
    p = jax.nn.softmax(jnp.where(mask, sc, -jnp.inf), axis=-1)
    o_lat = jnp.einsum('bhqk,bkc->bqhc', p, latent[..., :C_KV_LORA].astype(jnp.float32))
    return jnp.einsum('bqhc,chd->bqhd', o_lat, w_uv.astype(jnp.float32))


def mla_mixer(hp, hs, pos_p, pos_s, cache, page_table, w_down, q_lora_norm, w_uq, kv_norm, w_ukv, q_norm, k_norm, w_o):
    qp, lat_p = mla_project(hp, w_down, q_lora_norm, w_uq, kv_norm, q_norm, k_norm, pos_p)
    qs, lat_s = mla_project(hs, w_down, q_lora_norm, w_uq, kv_norm, q_norm, k_norm, pos_s)
    kn_p = mla_keys(lat_p, w_ukv, k_norm)
    op = causal_query_blocks(
        lambda qb, qpos: mla_attend(qb, kn_p, lat_p, w_ukv, pos_p[None, :] <= qpos[:, None]), qp)
    past = page_table.shape[1] * PAGE_SIZE
    smask = jnp.arange(past + hs.shape[1])[None, :] <= pos_s[:, None]

    def one_seq(a):
        q_b, r_b, pt = a
        lat = jnp.concatenate([cache[pt].reshape((past,) + cache.shape[2:]), r_b], axis=0)[None]
        return mla_attend(q_b[None], mla_keys(lat, w_ukv, k_norm), lat, w_ukv, smask)[0]

    osm = lax.map(one_seq, (qs, lat_s, page_table))
    yp = op.astype(hp.dtype).reshape(hp.shape[0], hp.shape[1], -1) @ w_o
    ys = osm.astype(hs.dtype).reshape(hs.shape[0], hs.shape[1], -1) @ w_o
    return yp, ys, lat_p, lat_s


def conv_ffn(h, ctx, w_gate, w_up, conv_w, conv_b, w_down):
    t = h.shape[1]
    gx = jnp.concatenate([ctx, h @ w_gate], axis=1)
    gc = conv_b + conv_w[0] * gx[:, :t]
    for j in range(1, CONV_W):
        gc = gc + conv_w[j] * gx[:, j:j + t]
    y = (jax.nn.silu(gc) * (h @ w_up)) @ w_down
    return y, gx[:, t:]


def setup_inputs(seed: int = 0) -> dict:
    key = jax.random.key(seed)
    keys = iter(jax.random.split(key, 48))

    def nrm(shape, scale=1.0):
        return jax.random.normal(next(keys), shape, jnp.float32) * scale

    def gain(shape):
        return 1.0 + 0.02 * nrm(shape)

    n_pages = PAST_LEN // PAGE_SIZE
    used = DEC_BATCH * n_pages
    n_pool = used + (used + 3) // 4
    page_table = jax.random.permutation(next(keys), n_pool)[:used].reshape(DEC_BATCH, n_pages).astype(jnp.int32)
    a_cols = len(A_GROUPS) * 3 * A_HEADS * A_HEAD_DIM
    b_cols = (B_HEADS + 2 * B_KV_HEADS) * 2 * B_HEAD_DIM
    c_down = C_Q_LORA + C_KV_LORA + C_ROPE
    return {
        'x_prompt': nrm((BATCH, SEQ, D_MODEL)),
        'x_sample': nrm((DEC_BATCH, DEC_SEQ, D_MODEL)),
        'cache_a_g0': nrm((N_A, DEC_BATCH, min(A_GROUPS[0][0], PAST_LEN), 2, A_HEADS, A_HEAD_DIM)),
        'cache_a_g1': nrm((N_A, DEC_BATCH, min(A_GROUPS[1][0], PAST_LEN), 2, A_HEADS, A_HEAD_DIM)),
        'cache_a_g2': nrm((N_A, DEC_BATCH, min(A_GROUPS[2][0], PAST_LEN), 2, A_HEADS, A_HEAD_DIM)),
        'cache_b_kv': nrm((N_B, n_pool, PAGE_SIZE, 2, B_KV_HEADS, 2 * B_HEAD_DIM)),
        'cache_c_latent': nrm((N_C, n_pool, PAGE_SIZE, C_KV_LORA + C_ROPE)),
        'state_ffn_conv': nrm((DEPTH, DEC_BATCH, CONV_W - 1, D_FF)),
        'page_table': page_table,
        'norm_attn': gain((DEPTH, D_MODEL)),
        'norm_ffn': gain((DEPTH, D_MODEL)),
        'a_w_qkv': nrm((N_A, D_MODEL, a_cols), D_MODEL ** -0.5),
        'a_q_norm': gain((N_A, len(A_GROUPS), A_HEAD_DIM)),
        'a_k_norm': gain((N_A, len(A_GROUPS), A_HEAD_DIM)),
        'a_w_o': nrm((N_A, A_HEADS * A_HEAD_DIM, D_MODEL), (A_HEADS * A_HEAD_DIM) ** -0.5),
        'b_w_qkv': nrm((N_B, D_MODEL, b_cols), D_MODEL ** -0.5),
        'b_q_norm': gain((N_B, B_HEAD_DIM)),
        'b_k_norm': gain((N_B, B_HEAD_DIM)),
        'b_lambda': nrm((N_B, 4, B_HEAD_DIM), 0.1),
        'b_subln': gain((N_B, 2 * B_HEAD_DIM)),
        'b_w_o': nrm((N_B, B_HEADS * 2 * B_HEAD_DIM, D_MODEL), (B_HEADS * 2 * B_HEAD_DIM) ** -0.5),
        'c_w_down': nrm((N_C, D_MODEL, c_down), D_MODEL ** -0.5),
        'c_q_lora_norm': gain((N_C, C_Q_LORA)),
        'c_w_uq': nrm((N_C, C_Q_LORA, C_HEADS * (C_NOPE + C_ROPE)), C_Q_LORA ** -0.5),
        'c_kv_norm': gain((N_C, C_KV_LORA)),
        'c_w_ukv': nrm((N_C, C_KV_LORA, C_HEADS * (C_NOPE + C_V)), C_KV_LORA ** -0.5),
        'c_q_norm': gain((N_C, C_NOPE + C_ROPE)),
        'c_k_norm': gain((N_C, C_NOPE + C_ROPE)),
        'c_w_o': nrm((N_C, C_HEADS * C_V, D_MODEL), (C_HEADS * C_V) ** -0.5),
        'ffn_w_gate': nrm((DEPTH, D_MODEL, D_FF), D_MODEL ** -0.5),
        'ffn_w_up': nrm((DEPTH, D_MODEL, D_FF), D_MODEL ** -0.5),
        'ffn_conv_w': nrm((DEPTH, CONV_W, D_FF), CONV_W ** -0.5),
        'ffn_conv_b': nrm((DEPTH, D_FF), 0.01),
        'ffn_w_down': nrm((DEPTH, D_FF, D_MODEL), D_FF ** -0.5),
    }


def reference(x_prompt, x_sample, cache_a_g0, cache_a_g1, cache_a_g2, cache_b_kv, cache_c_latent,
              state_ffn_conv, page_table, norm_attn, norm_ffn, a_w_qkv, a_q_norm, a_k_norm, a_w_o,
              b_w_qkv, b_q_norm, b_k_norm, b_lambda, b_subln, b_w_o, c_w_down, c_q_lora_norm, c_w_uq,
              c_kv_norm, c_w_ukv, c_q_norm, c_k_norm, c_w_o, ffn_w_gate, ffn_w_up, ffn_conv_w, ffn_conv_b,
              ffn_w_down):
    past = page_table.shape[1] * PAGE_SIZE
    pos_p = jnp.arange(x_prompt.shape[1])
    pos_s = past + jnp.arange(x_sample.shape[1])
    a_bufs = (cache_a_g0, cache_a_g1, cache_a_g2)
    n_groups = len(A_GROUPS)
    a_rows_p = [[] for _ in range(n_groups)]
    a_rows_s = [[] for _ in range(n_groups)]
    b_rows_p, b_rows_s, c_rows_p, c_rows_s, f_st_p, f_st_s = [], [], [], [], [], []
    xp, xs = x_prompt, x_sample
    ia = ib = ic = 0
    for i in range(DEPTH):
        hp, hs = rms_norm(xp, norm_attn[i]), rms_norm(xs, norm_attn[i])
        if i % N_MIXERS == 0:
            yp, ys, rp, rs = dilated_mixer(hp, hs, pos_p, pos_s, [buf[ia] for buf in a_bufs],
                                           a_w_qkv[ia], a_q_norm[ia], a_k_norm[ia], a_w_o[ia])
            for g in range(n_groups):
                a_rows_p[g].append(rp[g])
                a_rows_s[g].append(rs[g])
            ia += 1
        elif i % N_MIXERS == 1:
            yp, ys, rp, rs = diff_mixer(hp, hs, pos_p, pos_s, cache_b_kv[ib], page_table, i, b_w_qkv[ib],
                                        b_q_norm[ib], b_k_norm[ib], b_lambda[ib], b_subln[ib], b_w_o[ib])
            b_rows_p.append(rp)
            b_rows_s.append(rs)
            ib += 1
        else:
            yp, ys, rp, rs = mla_mixer(hp, hs, pos_p, pos_s, cache_c_latent[ic], page_table, c_w_down[ic],
                                       c_q_lora_norm[ic], c_w_uq[ic], c_kv_norm[ic], c_w_ukv[ic],
                                       c_q_norm[ic], c_k_norm[ic], c_w_o[ic])
            c_rows_p.append(rp)
            c_rows_s.append(rs)
            ic += 1
        xp, xs = xp + yp, xs + ys
        ctx_p = jnp.zeros((xp.shape[0], CONV_W - 1, D_FF), xp.dtype)
        fp, sp = conv_ffn(rms_norm(xp, norm_ffn[i]), ctx_p, ffn_w_gate[i], ffn_w_up[i],
                          ffn_conv_w[i], ffn_conv_b[i], ffn_w_down[i])
        fs, ss = conv_ffn(rms_norm(xs, norm_ffn[i]), state_ffn_conv[i], ffn_w_gate[i], ffn_w_up[i],
                          ffn_conv_w[i], ffn_conv_b[i], ffn_w_down[i])
        f_st_p.append(sp)
        f_st_s.append(ss)
        xp, xs = xp + fp, xs + fs
    return (xp, xs,
            jnp.stack(a_rows_p[0]), jnp.stack(a_rows_p[1]), jnp.stack(a_rows_p[2]),
            jnp.stack(b_rows_p), jnp.stack(c_rows_p), jnp.stack(f_st_p),
            jnp.stack(a_rows_s[0]), jnp.stack(a_rows_s[1]), jnp.stack(a_rows_s[2]),
            jnp.stack(b_rows_s), jnp.stack(c_rows_s), jnp.stack(f_st_s))
```

```python
import functools
import math

import numpy as np
import jax
import jax.numpy as jnp
from jax import lax
from jax.experimental import pallas as pl
from jax.experimental.pallas import tpu as pltpu

F32 = jnp.float32
BF16 = jnp.bfloat16

D_MODEL = 1024
PAGE_SIZE = 128
N_MIXERS = 3
ROPE_THETA = 10000.0
NORM_EPS = 1e-6

A_GROUPS = ((128, 1), (512, 4), (2048, 16))
A_HEADS = 8
A_HEAD_DIM = 64
A_SCALE = A_HEAD_DIM ** -0.5
A_BLOCK = 128
A_GW = A_HEADS * A_HEAD_DIM

B_HEADS = 8
B_KV_HEADS = 2
B_HEAD_DIM = 64
B_SCALE = B_HEAD_DIM ** -0.5
B_REP = B_HEADS // B_KV_HEADS

C_HEADS = 8
C_Q_LORA = 384
C_KV_LORA = 256
C_NOPE = 128
C_ROPE = 64
C_V = 128
C_SCALE = (C_NOPE + C_ROPE) ** -0.5
C_QK_PAD = 256
C_LAT = C_KV_LORA + C_ROPE

D_FF = 2816
CONV_W = 3
FFN_CHUNK = 256

ROPE_DIM = 64
ROPE_HALF = 32
LANE_CHUNK = 256
TOKEN_TILE = 512
FFN_SAMPLE_TILE = 256
FLASH_TILE = 512
PAGES_PER_STEP = 16
C_KEY_SUBTILE = 512
NEG_INF = float("-inf")
VMEM_LIMIT = 52 * 1024 * 1024


def _dot(a, b):
    return jnp.dot(a, b, preferred_element_type=F32)


def _dot_nt(a, b):
    return lax.dot_general(a, b, (((1,), (1,)), ((), ())), preferred_element_type=F32)


def _rms_rows(x, g):
    ms = jnp.mean(x * x, axis=-1, keepdims=True)
    return x * lax.rsqrt(ms + NORM_EPS) * g


def _seg_mean_sq(y, bmat):
    y2 = y * y
    hi = y2.astype(BF16)
    lo = (y2 - hi.astype(F32)).astype(BF16)
    return _dot(hi, bmat) + _dot(lo, bmat)


def _lo_half_mask(shape):
    return (lax.broadcasted_iota(jnp.int32, shape, len(shape) - 1) & (ROPE_DIM - 1)) < ROPE_HALF


def _swap_halves(y, lo_mask):
    n = y.shape[-1]
    return jnp.where(lo_mask, pltpu.roll(y, n - ROPE_HALF, 1), pltpu.roll(y, ROPE_HALF, 1))


def _norm_rope(y, bmat, gain, cos, sin, lo_mask):
    yn = y * lax.rsqrt(_seg_mean_sq(y, bmat) + NORM_EPS) * gain
    return yn * cos + _swap_halves(yn, lo_mask) * sin


def _const_spec(shape):
    nd = len(shape)
    return pl.BlockSpec(shape, lambda *_: (0,) * nd, pipeline_mode=pl.Buffered(1))


def _params(n_axes):
    return pltpu.CompilerParams(dimension_semantics=("arbitrary",) * n_axes,
                                vmem_limit_bytes=VMEM_LIMIT)


def _proj_kernel(modes, dests, x_ref, g_ref, w_ref, bmat_ref, gain_ref, cos_ref, sin_ref, *o_refs):
    h = _rms_rows(x_ref[...], g_ref[...]).astype(BF16)
    cos = cos_ref[...]
    sin = sin_ref[...]
    lo_mask = _lo_half_mask(cos.shape)
    for c, (mode, (oi, oc)) in enumerate(zip(modes, dests)):
        sl = slice(c * LANE_CHUNK, (c + 1) * LANE_CHUNK)
        y = _dot(h, w_ref[:, sl])
        if mode:
            y = _norm_rope(y, bmat_ref[...], gain_ref[:, sl], cos, sin, lo_mask)
        o_refs[oi][:, oc:oc + LANE_CHUNK] = y.astype(o_refs[oi].dtype)


def _proj_call(name, x, g, w, bmat, gain, cos, sin, modes, dests, outs, tm):
    t, d = x.shape
    n = w.shape[1]
    nrep = cos.shape[0] // tm
    kern = functools.partial(_proj_kernel, tuple(modes), tuple(dests))
    return pl.pallas_call(
        kern,
        grid=(t // tm,),
        in_specs=[pl.BlockSpec((tm, d), lambda i: (i, 0)),
                  _const_spec((1, d)), _const_spec((d, n)), _const_spec(bmat.shape), _const_spec((1, n)),
                  pl.BlockSpec((tm, LANE_CHUNK), lambda i: (i % nrep, 0)),
                  pl.BlockSpec((tm, LANE_CHUNK), lambda i: (i % nrep, 0))],
        out_specs=[pl.BlockSpec((tm, on), lambda i: (i, 0)) for on, _ in outs],
        out_shape=[jax.ShapeDtypeStruct((t, on), dt) for on, dt in outs],
        compiler_params=_params(1),
        name=name,
    )(x, g, w, bmat, gain, cos, sin)


def _c_proj_kernel(x_ref, g_ref, wd_ref, qln_ref, kvn_ref, kpg_ref, cosk_ref, sink_ref,
                   wuq_ref, bq_ref, gq_ref, cosq_ref, sinq_ref, wuk_ref, gk_ref, wuv_ref,
                   lat_ref, q_ref, k_ref, v_ref):
    h = _rms_rows(x_ref[...], g_ref[...]).astype(BF16)
    d = _dot(h, wd_ref[...])
    cq = _rms_rows(d[:, :C_Q_LORA], qln_ref[...])
    ckv = _rms_rows(d[:, C_Q_LORA:C_Q_LORA + C_KV_LORA], kvn_ref[...])
    kp = d[:, C_Q_LORA + C_KV_LORA:]
    ms = jnp.sum(kp * kp, axis=-1, keepdims=True) * (1.0 / C_ROPE)
    kpn = kp * lax.rsqrt(ms + NORM_EPS) * kpg_ref[...]
    kpe = kpn * cosk_ref[...] + _swap_halves(kpn, _lo_half_mask(kpn.shape)) * sink_ref[...]
    lat_ref[:, :C_KV_LORA] = ckv
    lat_ref[:, C_KV_LORA:] = kpe[:, :C_ROPE]
    cqb = cq.astype(BF16)
    ckvb = ckv.astype(BF16)
    kpeb = kpe.astype(BF16)
    cosq = cosq_ref[...]
    sinq = sinq_ref[...]
    lo_mask = _lo_half_mask(cosq.shape)
    for hh in range(C_HEADS):
        sq = slice(hh * C_QK_PAD, (hh + 1) * C_QK_PAD)
        yq = _dot(cqb, wuq_ref[:, sq])
        q_ref[:, sq] = _norm_rope(yq, bq_ref[...], gq_ref[...], cosq, sinq, lo_mask).astype(q_ref.dtype)
        yk = _dot(ckvb, wuk_ref[:, hh * C_NOPE:(hh + 1) * C_NOPE])
        k_ref[:, hh * C_QK_PAD:hh * C_QK_PAD + C_NOPE] = _rms_rows(yk, gk_ref[...]).astype(BF16)
        k_ref[:, hh * C_QK_PAD + C_NOPE:(hh + 1) * C_QK_PAD] = kpeb
    v_ref[...] = _dot(ckvb, wuv_ref[...]).astype(BF16)


def _c_proj_call(name, x, g, wd, qln, kvn, kpg, cosk, sink, wuq, bq, gq, cosq, sinq, wuk, gk, wuv, q_dtype, tm):
    t, d = x.shape
    nrep = cosk.shape[0] // tm
    tab = lambda w: pl.BlockSpec((tm, w), lambda i: (i % nrep, 0))
    row = lambda w: pl.BlockSpec((tm, w), lambda i: (i, 0))
    nq = C_HEADS * C_QK_PAD
    nv = C_HEADS * C_V
    return pl.pallas_call(
        _c_proj_kernel,
        grid=(t // tm,),
        in_specs=[row(d), _const_spec((1, d)), _const_spec(wd.shape), _const_spec(qln.shape),
                  _const_spec(kvn.shape), _const_spec(kpg.shape), tab(128), tab(128),
                  _const_spec(wuq.shape), _const_spec(bq.shape), _const_spec(gq.shape),
                  tab(LANE_CHUNK), tab(LANE_CHUNK),
                  _const_spec(wuk.shape), _const_spec(gk.shape), _const_spec(wuv.shape)],
        out_specs=[row(C_LAT), row(nq), row(nq), row(nv)],
        out_shape=[jax.ShapeDtypeStruct((t, C_LAT), F32), jax.ShapeDtypeStruct((t, nq), q_dtype),
                   jax.ShapeDtypeStruct((t, nq), BF16), jax.ShapeDtypeStruct((t, nv), BF16)],
        compiler_params=_params(1),
        name=name,
    )(x, g, wd, qln, kvn, kpg, cosk, sink, wuq, bq, gq, cosq, sinq, wuk, gk, wuv)


def _out_kernel(n_groups, *refs):
    x_ref, w_ref, y_ref = refs[-3:]
    if n_groups > 1:
        o_refs = refs[:n_groups]
        ls = [r[...] for r in refs[n_groups:2 * n_groups]]
        mx = functools.reduce(jnp.maximum, ls)
        es = [jnp.exp(l - mx) for l in ls]
        den = functools.reduce(lambda a, b: a + b, es)
        num = functools.reduce(lambda a, b: a + b, [e * r[...] for e, r in zip(es, o_refs)])
        a = (num / den).astype(BF16)
    else:
        a = refs[0][...].astype(BF16)
    y_ref[...] = x_ref[...] + _dot(a, w_ref[...])


def _out_call(name, a_list, l_list, x, w, tm):
    t, d = x.shape
    k = w.shape[0]
    row = lambda n: pl.BlockSpec((tm, n), lambda i: (i, 0))
    ins = list(a_list) + list(l_list)
    return pl.pallas_call(
        functools.partial(_out_kernel, len(a_list)),
        grid=(t // tm,),
        in_specs=[row(k)] * len(ins) + [row(d), _const_spec(w.shape)],
        out_specs=row(d),
        out_shape=jax.ShapeDtypeStruct((t, d), F32),
        compiler_params=_params(1),
        name=name,
    )(*ins, x, w)


def _ffn_chunks(h, x, prev_fn, wg_ref, wu_ref, cw_ref, cb_ref, wd_ref, o_ref, st_fn):
    o_ref[...] = x
    for c in range(D_FF // FFN_CHUNK):
        sl = slice(c * FFN_CHUNK, (c + 1) * FFN_CHUNK)
        gt = _dot(h, wg_ref[:, sl])
        up = _dot(h, wu_ref[:, sl])
        p1, p2 = prev_fn(gt, sl)
        gc = cb_ref[:, sl] + cw_ref[0:1, sl] * p2
        gc = gc + cw_ref[1:2, sl] * p1
        gc = gc + cw_ref[2:3, sl] * gt
        act = (gc * jax.nn.sigmoid(gc)) * up
        o_ref[...] += _dot(act.astype(BF16), wd_ref[sl, :])
        st_fn(gt, sl)


def _ffn_prompt_kernel(tiles_per_seq, x_ref, xh_ref, g_ref, wg_ref, wu_ref, cw_ref, cb_ref, wd_ref,
                       o_ref, st_ref):
    x = x_ref[...]
    tm = x.shape[0]
    g = g_ref[...]
    h = _rms_rows(x, g).astype(BF16)
    hh = _rms_rows(xh_ref[...], g).astype(BF16)
    keep = (pl.program_id(0) % tiles_per_seq != 0).astype(F32)

    def prev_fn(gt, sl):
        gh = _dot(hh, wg_ref[:, sl]) * keep
        full = jnp.concatenate([gh, gt], axis=0)
        return pltpu.roll(full, 1, 0)[8:], pltpu.roll(full, 2, 0)[8:]

    def st_fn(gt, sl):
        st_ref[0, :, sl] = gt[tm - 8:]

    _ffn_chunks(h, x, prev_fn, wg_ref, wu_ref, cw_ref, cb_ref, wd_ref, o_ref, st_fn)


def _ffn_sample_kernel(seq, x_ref, e1_ref, e2_ref, g_ref, wg_ref, wu_ref, cw_ref, cb_ref, wd_ref,
                       o_ref, st_ref):
    x = x_ref[...]
    tm = x.shape[0]
    h = _rms_rows(x, g_ref[...]).astype(BF16)
    tpos = lax.broadcasted_iota(jnp.int32, (tm, 1), 0) & (seq - 1)

    def prev_fn(gt, sl):
        p1 = jnp.where(tpos >= 1, pltpu.roll(gt, 1, 0), e1_ref[:, sl])
        p2 = jnp.where(tpos >= 2, pltpu.roll(gt, 2, 0), e2_ref[:, sl])
        return p1, p2

    def st_fn(gt, sl):
        st_ref[:, sl] = gt

    _ffn_chunks(h, x, prev_fn, wg_ref, wu_ref, cw_ref, cb_ref, wd_ref, o_ref, st_fn)


def _ffn_weight_specs(wg, wu, cw, cb, wd):
    return [_const_spec(wg.shape), _const_spec(wu.shape), _const_spec(cw.shape), _const_spec(cb.shape),
            _const_spec(wd.shape)]


def _ffn_prompt_call(name, x, seq_len, g, wg, wu, cw, cb, wd, tm):
    t, d = x.shape
    tps = seq_len // tm
    hb = tm // 8
    return pl.pallas_call(
        functools.partial(_ffn_prompt_kernel, tps),
        grid=(t // tm,),
        in_specs=[pl.BlockSpec((tm, d), lambda i: (i, 0)),
                  pl.BlockSpec((8, d), lambda i: (jnp.maximum(i * hb - 1, 0), 0)),
                  _const_spec((1, d))] + _ffn_weight_specs(wg, wu, cw, cb, wd),
        out_specs=[pl.BlockSpec((tm, d), lambda i: (i, 0)),
                   pl.BlockSpec((1, 8, D_FF), lambda i: (i // tps, 0, 0))],
        out_shape=[jax.ShapeDtypeStruct((t, d), F32),
                   jax.ShapeDtypeStruct((t // seq_len, 8, D_FF), F32)],
        compiler_params=_params(1),
        name=name,
    )(x, x, g, wg, wu, cw, cb, wd)


def _ffn_sample_call(name, x, seq, e1, e2, g, wg, wu, cw, cb, wd, tm):
    t, d = x.shape
    row = lambda n: pl.BlockSpec((tm, n), lambda i: (i, 0))
    return pl.pallas_call(
        functools.partial(_ffn_sample_kernel, seq),
        grid=(t // tm,),
        in_specs=[row(d), row(D_FF), row(D_FF), _const_spec((1, d))] + _ffn_weight_specs(wg, wu, cw, cb, wd),
        out_specs=[row(d), row(D_FF)],
        out_shape=[jax.ShapeDtypeStruct((t, d), F32), jax.ShapeDtypeStruct((t, D_FF), F32)],
        compiler_params=_params(1),
        name=name,
    )(x, e1, e2, g, wg, wu, cw, cb, wd)


def _a_prompt_kernel(q_ref, kp_ref, kc_ref, vp_ref, vc_ref, o_ref, lse_ref):
    nblk = pl.program_id(2)
    q = q_ref[0].astype(BF16)
    kk = jnp.concatenate([kp_ref[0], kc_ref[0]], axis=0).astype(BF16)
    vv = jnp.concatenate([vp_ref[0], vc_ref[0]], axis=0).astype(BF16)
    qi = lax.broadcasted_iota(jnp.int32, (A_BLOCK, 2 * A_BLOCK), 0) + A_BLOCK
    ki = lax.broadcasted_iota(jnp.int32, (A_BLOCK, 2 * A_BLOCK), 1)
    dlt = qi - ki
    first_key = jnp.where(nblk > 0, 0, A_BLOCK)
    mask = (dlt >= 0) & (dlt <= A_BLOCK) & (ki >= first_key)
    lo_lane = lax.broadcasted_iota(jnp.int32, (1, 2 * A_HEAD_DIM), 1) < A_HEAD_DIM
    for p in range(A_HEADS // 2):
        sl = slice(p * 2 * A_HEAD_DIM, (p + 1) * 2 * A_HEAD_DIM)
        qp, kpair, vpair = q[:, sl], kk[:, sl], vv[:, sl]
        o_pair = jnp.zeros((A_BLOCK, 2 * A_HEAD_DIM), F32)
        lse_pair = jnp.zeros((A_BLOCK, 2 * A_HEAD_DIM), F32)
        for half in range(2):
            hm = lo_lane if half == 0 else jnp.logical_not(lo_lane)
            s = _dot_nt(jnp.where(hm, qp, jnp.zeros_like(qp)), kpair)
            s = jnp.where(mask, s, NEG_INF)
            m = jnp.max(s, axis=-1, keepdims=True)
            e = jnp.exp(s - m)
            l = jnp.sum(e, axis=-1, keepdims=True)
            pv = _dot(e.astype(BF16), jnp.where(hm, vpair, jnp.zeros_like(vpair)))
            o_pair = o_pair + pv / l
            lse_pair = jnp.where(hm, m + jnp.log(l), lse_pair)
        o_ref[0, :, sl] = o_pair
        lse_ref[0, :, sl] = lse_pair


def _a_prompt_call(name, qkv, batch, seq, g, dil):
    cols = qkv.shape[1]
    nsec = cols // A_GW
    m_len = seq // dil
    nb = m_len // A_BLOCK
    view = qkv.reshape(batch, m_len, dil * cols)
    cq = lambda r: r * nsec + g * 3
    blk = (1, A_BLOCK, A_GW)
    cur = lambda off: pl.BlockSpec(blk, lambda b, r, n: (b, n, cq(r) + off))
    prev = lambda off: pl.BlockSpec(blk, lambda b, r, n: (b, jnp.maximum(n - 1, 0), cq(r) + off))
    o_spec = pl.BlockSpec(blk, lambda b, r, n: (b, n, r))
    o, lse = pl.pallas_call(
        _a_prompt_kernel,
        grid=(batch, dil, nb),
        in_specs=[cur(0), prev(1), cur(1), prev(2), cur(2)],
        out_specs=[o_spec, o_spec],
        out_shape=[jax.ShapeDtypeStruct((batch, m_len, dil * A_GW), F32)] * 2,
        compiler_params=_params(3),
        name=name,
    )(view, view, view, view, view)
    return o.reshape(batch * seq, A_GW), lse.reshape(batch * seq, A_GW)


def _a_sample_kernel(seq, qkv_ref, c0_ref, c1_ref, c2_ref, o_ref):
    rows = A_HEADS * seq
    qkv = qkv_ref[0]
    blockmask = _own_block_mask((rows, A_GW), seq, A_HEAD_DIM)
    outs, lses = [], []
    for g, ((window, dil), c_ref) in enumerate(zip(A_GROUPS, (c0_ref, c1_ref, c2_ref))):
        base = g * 3 * A_GW
        q = qkv[:, base:base + A_GW]
        kn = qkv[:, base + A_GW:base + 2 * A_GW]
        vn = qkv[:, base + 2 * A_GW:base + 3 * A_GW]
        qblk = jnp.where(blockmask, jnp.concatenate([q] * A_HEADS, axis=0), 0.0).astype(BF16)
        buf_len = c_ref.shape[2]
        kc = c_ref[0, 0, :, :A_GW].astype(BF16)
        vc = c_ref[0, 0, :, A_GW:].astype(BF16)
        s = _dot_nt(qblk, kc)
        kidx = lax.broadcasted_iota(jnp.int32, (rows, buf_len), 1)
        trow = lax.broadcasted_iota(jnp.int32, (rows, buf_len), 0) & (seq - 1)
        back = buf_len + trow - kidx
        valid = (back <= window) & ((back & (dil - 1)) == 0)
        s = jnp.where(valid, s, NEG_INF)
        zpad = jnp.zeros((A_BLOCK - seq, A_GW), F32)
        knp = jnp.concatenate([kn, zpad], axis=0).astype(BF16)
        vnp = jnp.concatenate([vn, zpad], axis=0).astype(BF16)
        s2 = _dot_nt(qblk, knp)
        u = lax.broadcasted_iota(jnp.int32, (rows, A_BLOCK), 1)
        t2 = lax.broadcasted_iota(jnp.int32, (rows, A_BLOCK), 0) & (seq - 1)
        valid2 = (u <= t2) & (((t2 - u) & (dil - 1)) == 0)
        s2 = jnp.where(valid2, s2, NEG_INF)
        m = jnp.maximum(jnp.max(s, axis=-1, keepdims=True), jnp.max(s2, axis=-1, keepdims=True))
        e = jnp.exp(s - m)
        e2 = jnp.exp(s2 - m)
        l = jnp.sum(e, axis=-1, keepdims=True) + jnp.sum(e2, axis=-1, keepdims=True)
        outs.append((_dot(e.astype(BF16), vc) + _dot(e2.astype(BF16), vnp)) / l)
        lses.append(m + jnp.log(l))
    mx = functools.reduce(jnp.maximum, lses)
    es = [jnp.exp(l - mx) for l in lses]
    den = es[0] + es[1] + es[2]
    merged = (es[0] * outs[0] + es[1] * outs[1] + es[2] * outs[2]) / den
    merged = jnp.where(blockmask, merged, 0.0)
    o = merged[0:seq]
    for hh in range(1, A_HEADS):
        o = o + merged[hh * seq:(hh + 1) * seq]
    o_ref[0] = o


def _a_sample_call(name, qkv, caches, layer, dec_batch, seq):
    cols = qkv.shape[1]
    views = [c.reshape(c.shape[0], c.shape[1], c.shape[2], 2 * A_GW) for c in caches]
    cspec = lambda v: pl.BlockSpec((1, 1, v.shape[2], 2 * A_GW), lambda b: (layer, b, 0, 0))
    return pl.pallas_call(
        functools.partial(_a_sample_kernel, seq),
        grid=(dec_batch,),
        in_specs=[pl.BlockSpec((1, seq, cols), lambda b: (b, 0, 0))] + [cspec(v) for v in views],
        out_specs=pl.BlockSpec((1, seq, A_GW), lambda b: (b, 0, 0)),
        out_shape=jax.ShapeDtypeStruct((dec_batch, seq, A_GW), F32),
        compiler_params=_params(1),
        name=name,
    )(qkv.reshape(dec_batch, seq, cols), *views)


def _flash_kernel(k_slices, v_slices, vheads, finalize, q_ref, k_ref, v_ref, *rest):
    extra = rest[:-4]
    o_ref, m_scr, l_scr, acc_scr = rest[-4:]
    i = pl.program_id(1)
    j = pl.program_id(2)
    tq = q_ref.shape[1]
    tk = k_ref.shape[1]

    @pl.when(j == 0)
    def _():
        m_scr[...] = jnp.full(m_scr.shape, NEG_INF, F32)
        l_scr[...] = jnp.zeros(l_scr.shape, F32)
        acc_scr[...] = jnp.zeros(acc_scr.shape, F32)

    def step(masked):
        ks = [k_ref[0, :, sl].astype(BF16) for sl in k_slices]
        vs = [v_ref[0, :, sl].astype(BF16) for sl in v_slices]
        if masked:
            causal = (lax.broadcasted_iota(jnp.int32, (tq, tk), 0)
                      >= lax.broadcasted_iota(jnp.int32, (tq, tk), 1))
        for vh, (qs, comp, kidx, vidx) in enumerate(vheads):
            q = q_ref[0, :, qs]
            if comp is not None:
                lane = lax.broadcasted_iota(jnp.int32, (1, q.shape[1]), 1)
                keep = (lane < B_HEAD_DIM) if comp == 0 else (lane >= B_HEAD_DIM)
                q = jnp.where(keep, q, jnp.zeros_like(q))
            s = _dot_nt(q, ks[kidx])
            if masked:
                s = jnp.where(causal, s, NEG_INF)
            m_prev = m_scr[vh]
            m_new = jnp.maximum(m_prev, jnp.max(s, axis=-1, keepdims=True))
            alpha = jnp.exp(m_prev - m_new)
            p = jnp.exp(s - m_new)
            l_scr[vh] = alpha * l_scr[vh] + jnp.sum(p, axis=-1, keepdims=True)
            acc_scr[vh] = alpha * acc_scr[vh] + _dot(p.astype(BF16), vs[vidx])
            m_scr[vh] = m_new

    @pl.when(j < i)
    def _():
        step(False)

    @pl.when(j == i)
    def _():
        step(True)
        finalize(extra, o_ref, l_scr, acc_scr)


def _b_finalize(lam_init, extra, o_ref, l_scr, acc_scr):
    lam_ref, subln_ref = extra
    lam = lam_ref[...]
    for hh in range(B_HEADS):
        o1 = acc_scr[2 * hh] / l_scr[2 * hh]
        o2 = acc_scr[2 * hh + 1] / l_scr[2 * hh + 1]
        od = _rms_rows(o1 - lam * o2, subln_ref[...]) * (1.0 - lam_init)
        o_ref[0, :, hh * 2 * B_HEAD_DIM:(hh + 1) * 2 * B_HEAD_DIM] = od.astype(o_ref.dtype)


def _c_finalize(extra, o_ref, l_scr, acc_scr):
    for hh in range(C_HEADS):
        o_ref[0, :, hh * C_V:(hh + 1) * C_V] = (acc_scr[hh] / l_scr[hh]).astype(o_ref.dtype)


def _flash_call(name, q, k, v, extra, batch, seq, k_slices, v_slices, vheads, finalize, dv, n_out):
    tq = min(FLASH_TILE, seq)
    nq = seq // tq
    qv = q.reshape(batch, seq, q.shape[1])
    kv = k.reshape(batch, seq, k.shape[1])
    vv = v.reshape(batch, seq, v.shape[1])
    nvh = len(vheads)
    kern = functools.partial(_flash_kernel, tuple(k_slices), tuple(v_slices), tuple(vheads), finalize)
    causal_blk = lambda n: pl.BlockSpec((1, tq, n), lambda b, i, j: (b, jnp.minimum(i, j), 0))
    out = pl.pallas_call(
        kern,
        grid=(batch, nq, nq),
        in_specs=[pl.BlockSpec((1, tq, qv.shape[2]), lambda b, i, j: (b, i, 0)),
                  causal_blk(kv.shape[2]), causal_blk(vv.shape[2])]
                 + [_const_spec(e.shape) for e in extra],
        out_specs=pl.BlockSpec((1, tq, n_out), lambda b, i, j: (b, i, 0)),
        out_shape=jax.ShapeDtypeStruct((batch, seq, n_out), BF16),
        scratch_shapes=[pltpu.VMEM((nvh, tq, 1), F32), pltpu.VMEM((nvh, tq, 1), F32),
                        pltpu.VMEM((nvh, tq, dv), F32)],
        compiler_params=_params(3),
        name=name,
    )(qv, kv, vv, *extra)
    return out.reshape(batch * seq, n_out)


def _online_update(s, m_scr, l_scr):
    m_prev = m_scr[...]
    m_new = jnp.maximum(m_prev, jnp.max(s, axis=-1, keepdims=True))
    alpha = jnp.exp(m_prev - m_new)
    p = jnp.exp(s - m_new)
    l_scr[...] = alpha * l_scr[...] + jnp.sum(p, axis=-1, keepdims=True)
    m_scr[...] = m_new
    return alpha, p.astype(BF16)


def _new_row_mask(rows, seq):
    u = lax.broadcasted_iota(jnp.int32, (rows, PAGE_SIZE), 1)
    t = lax.broadcasted_iota(jnp.int32, (rows, PAGE_SIZE), 0) & (seq - 1)
    return u <= t


def _own_block_mask(shape, rows_per_head, cols_per_head):
    row_head = lax.broadcasted_iota(jnp.int32, shape, 0) >> int(math.log2(rows_per_head))
    col_head = lax.broadcasted_iota(jnp.int32, shape, 1) >> int(math.log2(cols_per_head))
    return row_head == col_head


def _b_sample_kernel(n_pages, seq, lam_init, pt_ref, q_ref, kvn_ref, lam_ref, subln_ref, *rest):
    page_refs = rest[:n_pages]
    o_ref, qall_scr, kv_scr, m_scr, l_scr, acc_scr = rest[n_pages:]
    j = pl.program_id(1)
    width = 2 * B_HEAD_DIM
    rows = B_KV_HEADS * 2 * B_REP * seq
    half_rows = rows // B_KV_HEADS
    kw = B_KV_HEADS * width

    @pl.when(j == 0)
    def _():
        q = q_ref[0]
        lane = lax.broadcasted_iota(jnp.int32, (1, width), 1)
        zero = jnp.zeros((seq, width), F32)
        pieces = []
        for n in range(B_KV_HEADS):
            for c in range(2):
                keep = (lane < B_HEAD_DIM) if c == 0 else (lane >= B_HEAD_DIM)
                for g in range(B_REP):
                    hh = n * B_REP + g
                    qp = jnp.where(keep, q[:, hh * width:(hh + 1) * width], 0.0)
                    parts = [zero] * B_KV_HEADS
                    parts[n] = qp
                    pieces.append(jnp.concatenate(parts, axis=1))
        qall_scr[...] = jnp.concatenate(pieces, axis=0).astype(BF16)
        m_scr[...] = jnp.full(m_scr.shape, NEG_INF, F32)
        l_scr[...] = jnp.zeros(l_scr.shape, F32)
        acc_scr[...] = jnp.zeros(acc_scr.shape, F32)

    for p in range(n_pages):
        kv_scr[p * PAGE_SIZE:(p + 1) * PAGE_SIZE, :] = page_refs[p][0].astype(BF16)

    first_kv_head = lax.broadcasted_iota(jnp.int32, (rows, 1), 0) < half_rows

    def update(kt, vt, mask):
        s = _dot_nt(qall_scr[...], kt)
        if mask is not None:
            s = jnp.where(mask, s, NEG_INF)
        alpha, p = _online_update(s, m_scr, l_scr)
        r = _dot(p, vt)
        acc_scr[...] = alpha * acc_scr[...] + jnp.where(first_kv_head, r[:, :width], r[:, width:])

    update(kv_scr[:, :kw], kv_scr[:, kw:], None)

    @pl.when(j == pl.num_programs(1) - 1)
    def _():
        zpad = jnp.zeros((PAGE_SIZE - seq, 2 * kw), F32)
        new = jnp.concatenate([kvn_ref[0], zpad], axis=0).astype(BF16)
        update(new[:, :kw], new[:, kw:], _new_row_mask(rows, seq))
        o = acc_scr[...] / l_scr[...]
        lam = lam_ref[...]
        grp = B_REP * seq
        for n in range(B_KV_HEADS):
            o1 = o[n * 2 * grp:n * 2 * grp + grp]
            o2 = o[n * 2 * grp + grp:(n + 1) * 2 * grp]
            od = _rms_rows(o1 - lam * o2, subln_ref[...]) * (1.0 - lam_init)
            for g in range(B_REP):
                hh = n * B_REP + g
                o_ref[0, :, hh * width:(hh + 1) * width] = od[g * seq:(g + 1) * seq]


def _page_specs(n_pages, pages_per_seq, width):
    def spec(p):
        return pl.BlockSpec((1, PAGE_SIZE, width),
                            lambda b, j, pt: (pt[b * pages_per_seq + j * n_pages + p], 0, 0))
    return [spec(p) for p in range(n_pages)]


def _b_sample_call(name, q, kvn, lam, subln, cache, pt_flat, dec_batch, seq, pages_per_seq, lam_init):
    n_pages = min(PAGES_PER_STEP, pages_per_seq)
    width = cache.shape[2]
    rows = B_KV_HEADS * 2 * B_REP * seq
    per_seq = lambda n: pl.BlockSpec((1, seq, n), lambda b, j, pt: (b, 0, 0))
    grid_spec = pltpu.PrefetchScalarGridSpec(
        num_scalar_prefetch=1,
        grid=(dec_batch, pages_per_seq // n_pages),
        in_specs=[per_seq(q.shape[1]), per_seq(width),
                  pl.BlockSpec(lam.shape, lambda b, j, pt: (0, 0)),
                  pl.BlockSpec(subln.shape, lambda b, j, pt: (0, 0))]
                 + _page_specs(n_pages, pages_per_seq, width),
        out_specs=per_seq(B_HEADS * 2 * B_HEAD_DIM),
        scratch_shapes=[pltpu.VMEM((rows, B_KV_HEADS * 2 * B_HEAD_DIM), BF16),
                        pltpu.VMEM((n_pages * PAGE_SIZE, width), BF16),
                        pltpu.VMEM((rows, 1), F32), pltpu.VMEM((rows, 1), F32),
                        pltpu.VMEM((rows, 2 * B_HEAD_DIM), F32)])
    out = pl.pallas_call(
        functools.partial(_b_sample_kernel, n_pages, seq, lam_init),
        grid_spec=grid_spec,
        out_shape=jax.ShapeDtypeStruct((dec_batch, seq, B_HEADS * 2 * B_HEAD_DIM), F32),
        compiler_params=_params(2),
        name=name,
    )(pt_flat, q.reshape(dec_batch, seq, -1), kvn.reshape(dec_batch, seq, -1), lam, subln,
      *([cache] * n_pages))
    return out.reshape(dec_batch * seq, -1)


def _c_sample_kernel(n_pages, seq, pt_ref, q_ref, latn_ref, gk_ref, wukt_ref, wuk_ref, wuv_ref, *rest):
    page_refs = rest[:n_pages]
    o_ref, qp_scr, qr_scr, lat_scr, m_scr, l_scr, acc_scr = rest[n_pages:]
    j = pl.program_id(1)
    rows = C_HEADS * seq

    @pl.when(j == 0)
    def _():
        q = q_ref[0]
        zero = jnp.zeros((seq, C_NOPE), F32)
        blocks, ropes = [], []
        for hh in range(C_HEADS):
            parts = [zero] * C_HEADS
            parts[hh] = q[:, hh * C_QK_PAD:hh * C_QK_PAD + C_NOPE] * gk_ref[...]
            blocks.append(jnp.concatenate(parts, axis=1))
            ropes.append(q[:, hh * C_QK_PAD + C_NOPE:hh * C_QK_PAD + C_NOPE + C_ROPE])
        qn = jnp.concatenate(blocks, axis=0).astype(BF16)
        qp_scr[...] = _dot_nt(qn, wuk_ref[...]).astype(BF16)
        qr_scr[...] = jnp.concatenate(ropes, axis=0).astype(BF16)
        m_scr[...] = jnp.full(m_scr.shape, NEG_INF, F32)
        l_scr[...] = jnp.zeros(l_scr.shape, F32)
        acc_scr[...] = jnp.zeros(acc_scr.shape, F32)

    for p in range(n_pages):
        lat_scr[p * PAGE_SIZE:(p + 1) * PAGE_SIZE, :] = page_refs[p][0].astype(BF16)

    def tile(lt, mask):
        ckv = lt[:, :C_KV_LORA]
        kpe = lt[:, C_KV_LORA:]
        nk = lt.shape[0]
        kxt = _dot_nt(wukt_ref[...], ckv)
        kx2 = kxt * kxt
        scale_rows = []
        for hh in range(C_HEADS):
            ms = jnp.sum(kx2[hh * C_NOPE:(hh + 1) * C_NOPE], axis=0, keepdims=True) * (1.0 / C_NOPE)
            scale_rows.append(jnp.broadcast_to(lax.rsqrt(ms + NORM_EPS), (seq, nk)))
        s = _dot_nt(qp_scr[...], ckv) * jnp.concatenate(scale_rows, axis=0) + _dot_nt(qr_scr[...], kpe)
        if mask is not None:
            s = jnp.where(mask, s, NEG_INF)
        alpha, p = _online_update(s, m_scr, l_scr)
        acc_scr[...] = alpha * acc_scr[...] + _dot(p, ckv)

    sub = min(C_KEY_SUBTILE, n_pages * PAGE_SIZE)
    for t in range(n_pages * PAGE_SIZE // sub):
        tile(lat_scr[t * sub:(t + 1) * sub, :], None)

    @pl.when(j == pl.num_programs(1) - 1)
    def _():
        zpad = jnp.zeros((PAGE_SIZE - seq, C_LAT), F32)
        tile(jnp.concatenate([latn_ref[0], zpad], axis=0).astype(BF16), _new_row_mask(rows, seq))
        o_lat = (acc_scr[...] / l_scr[...]).astype(BF16)
        r = _dot(o_lat, wuv_ref[...])
        nv = C_HEADS * C_V
        r = jnp.where(_own_block_mask((rows, nv), seq, C_V), r, 0.0)
        o = r[0:seq]
        for hh in range(1, C_HEADS):
            o = o + r[hh * seq:(hh + 1) * seq]
        o_ref[0] = o


def _c_sample_call(name, q, latn, gk, wukt, wuk, wuv, cache, pt_flat, dec_batch, seq, pages_per_seq):
    n_pages = min(PAGES_PER_STEP, pages_per_seq)
    rows = C_HEADS * seq
    per_seq = lambda n: pl.BlockSpec((1, seq, n), lambda b, j, pt: (b, 0, 0))
    const = lambda a: pl.BlockSpec(a.shape, lambda b, j, pt: (0, 0))
    grid_spec = pltpu.PrefetchScalarGridSpec(
        num_scalar_prefetch=1,
        grid=(dec_batch, pages_per_seq // n_pages),
        in_specs=[per_seq(q.shape[1]), per_seq(C_LAT), const(gk), const(wukt), const(wuk), const(wuv)]
                 + _page_specs(n_pages, pages_per_seq, C_LAT),
        out_specs=per_seq(C_HEADS * C_V),
        scratch_shapes=[pltpu.VMEM((rows, C_KV_LORA), BF16), pltpu.VMEM((rows, C_ROPE), BF16),
                        pltpu.VMEM((n_pages * PAGE_SIZE, C_LAT), BF16),
                        pltpu.VMEM((rows, 1), F32), pltpu.VMEM((rows, 1), F32),
                        pltpu.VMEM((rows, C_KV_LORA), F32)])
    out = pl.pallas_call(
        functools.partial(_c_sample_kernel, n_pages, seq),
        grid_spec=grid_spec,
        out_shape=jax.ShapeDtypeStruct((dec_batch, seq, C_HEADS * C_V), F32),
        compiler_params=_params(2),
        name=name,
    )(pt_flat, q.reshape(dec_batch, seq, -1), latn.reshape(dec_batch, seq, -1), gk, wukt, wuk, wuv,
      *([cache] * n_pages))
    return out.reshape(dec_batch * seq, -1)


def _block_mean_matrix(widths):
    m = np.zeros((LANE_CHUNK, LANE_CHUNK), np.float32)
    o = 0
    for w in widths:
        m[o:o + w, o:o + w] = 1.0 / w
        o += w
    return jnp.asarray(m, BF16)


def _rope_tables(pos):
    inv = ROPE_THETA ** (-jnp.arange(ROPE_HALF, dtype=F32) / ROPE_HALF)
    ang = pos.astype(F32)[:, None] * inv[None, :]
    cos, sin = jnp.cos(ang), jnp.sin(ang)
    return jnp.concatenate([cos, cos], axis=1), jnp.concatenate([-sin, sin], axis=1)


def _tables(pos):
    cos64, sin64 = _rope_tables(pos)
    n = pos.shape[0]
    one, zero = jnp.ones((n, ROPE_DIM), F32), jnp.zeros((n, ROPE_DIM), F32)
    return dict(
        full=(jnp.tile(cos64, (1, 4)), jnp.tile(sin64, (1, 4))),
        cq=(jnp.concatenate([one, one, cos64, one], 1), jnp.concatenate([zero, zero, sin64, zero], 1)),
        ck=(jnp.concatenate([cos64, one], 1), jnp.concatenate([sin64, zero], 1)))


def kernel(x_prompt, x_sample, cache_a_g0, cache_a_g1, cache_a_g2, cache_b_kv, cache_c_latent,
           state_ffn_conv, page_table, norm_attn, norm_ffn, a_w_qkv, a_q_norm, a_k_norm, a_w_o,
           b_w_qkv, b_q_norm, b_k_norm, b_lambda, b_subln, b_w_o, c_w_down, c_q_lora_norm, c_w_uq,
           c_kv_norm, c_w_ukv, c_q_norm, c_k_norm, c_w_o, ffn_w_gate, ffn_w_up, ffn_conv_w, ffn_conv_b,
           ffn_w_down):
    batch, seq, d = x_prompt.shape
    dec_batch, dec_seq, _ = x_sample.shape
    depth = norm_attn.shape[0]
    pages_per_seq = page_table.shape[1]
    past = pages_per_seq * PAGE_SIZE
    n_pool = cache_b_kv.shape[1] if cache_b_kv.shape[0] else cache_c_latent.shape[1]
    a_caches = (cache_a_g0, cache_a_g1, cache_a_g2)
    for (window, dil), c in zip(A_GROUPS, a_caches):
        assert c.shape[2] == window and seq % (dil * A_BLOCK) == 0
    assert dec_seq == 8 and A_BLOCK == PAGE_SIZE

    tp, ts = batch * seq, dec_batch * dec_seq
    tm_p, tm_s = min(TOKEN_TILE, seq), min(TOKEN_TILE, ts)
    xp = x_prompt.reshape(tp, d)
    xs = x_sample.reshape(ts, d)
    tab_p = _tables(jnp.arange(seq))
    tab_s = _tables(jnp.tile(past + jnp.arange(dec_seq), tm_s // dec_seq))
    pt_flat = page_table.reshape(-1)
    bmat64 = _block_mean_matrix((64, 64, 64, 64))
    bmat_cq = _block_mean_matrix((128, 64, 64))
    row = lambda v: v.reshape(1, -1).astype(F32)

    a_p = [[] for _ in A_GROUPS]
    a_s = [[] for _ in A_GROUPS]
    b_p, b_s, c_p, c_s, f_p, f_s = [], [], [], [], [], []
    ia = ib = ic = 0
    for i in range(depth):
        g_attn = row(norm_attn[i])
        if i % N_MIXERS == 0:
            w = a_w_qkv[ia].astype(BF16)
            ncol = w.shape[1]
            gains = []
            for g in range(len(A_GROUPS)):
                gains += [jnp.tile(a_q_norm[ia, g], A_HEADS) * A_SCALE, jnp.tile(a_k_norm[ia, g], A_HEADS),
                          jnp.ones((A_GW,), F32)]
            gain = row(jnp.concatenate(gains))
            nch = ncol // LANE_CHUNK
            modes = [(c % 6) < 4 for c in range(nch)]
            dests = [(0, c * LANE_CHUNK) for c in range(nch)]
            proj = lambda nm, x, tab, tm: _proj_call(
                nm, x, g_attn, w, bmat64, gain, tab['full'][0], tab['full'][1], modes, dests,
                [(ncol, F32)], tm)[0]
            qkv_p = proj(f"a{ia}_proj_p", xp, tab_p, tm_p)
            qkv_s = proj(f"a{ia}_proj_s", xs, tab_s, tm_s)
            outs, lses = [], []
            for g, (window, dil) in enumerate(A_GROUPS):
                o, l = _a_prompt_call(f"a{ia}_attn_p{g}", qkv_p, batch, seq, g, dil)
                outs.append(o)
                lses.append(l)
                keep = min(window, seq)
                r = qkv_p.reshape(batch, seq, len(A_GROUPS), 3, A_HEADS, A_HEAD_DIM)
                a_p[g].append(r[:, seq - keep:, g, 1:3])
                a_s[g].append(qkv_s.reshape(dec_batch, dec_seq, len(A_GROUPS), 3, A_HEADS, A_HEAD_DIM)[:, :, g, 1:3])
            wo = a_w_o[ia].astype(BF16)
            xp = _out_call(f"a{ia}_out_p", outs, lses, xp, wo, tm_p)
            os_ = _a_sample_call(f"a{ia}_attn_s", qkv_s, a_caches, ia, dec_batch, dec_seq)
            xs = _out_call(f"a{ia}_out_s", [os_.reshape(ts, A_GW)], [], xs, wo, tm_s)
            ia += 1
        elif i % N_MIXERS == 1:
            lam_init = 0.8 - 0.6 * math.exp(-0.3 * i)
            lpf = b_lambda[ib].astype(F32)
            lam = (jnp.exp(jnp.sum(lpf[0] * lpf[1])) - jnp.exp(jnp.sum(lpf[2] * lpf[3])) + lam_init).reshape(1, 1)
            w = b_w_qkv[ib].astype(BF16)
            nq = B_HEADS * 2 * B_HEAD_DIM
            nk = B_KV_HEADS * 2 * B_HEAD_DIM
            gain = row(jnp.concatenate([jnp.tile(b_q_norm[ib], 2 * B_HEADS) * B_SCALE,
                                        jnp.tile(b_k_norm[ib], 2 * B_KV_HEADS), jnp.ones((nk,), F32)]))
            nch = (nq + 2 * nk) // LANE_CHUNK
            nqc = nq // LANE_CHUNK
            nkc = nk // LANE_CHUNK
            modes = [True] * (nqc + nkc) + [False] * nkc
            dests = [(0, c * LANE_CHUNK) if c < nqc else (1, (c - nqc) * LANE_CHUNK) for c in range(nch)]
            proj = lambda nm, x, tab, tm, qdt: _proj_call(
                nm, x, g_attn, w, bmat64, gain, tab['full'][0], tab['full'][1], modes, dests,
                [(nq, qdt), (2 * nk, F32)], tm)
            q_p, kv_p = proj(f"b{ib}_proj_p", xp, tab_p, tm_p, BF16)
            q_s, kv_s = proj(f"b{ib}_proj_s", xs, tab_s, tm_s, F32)
            b_p.append(kv_p.reshape(batch, seq, 2, B_KV_HEADS, 2 * B_HEAD_DIM))
            b_s.append(kv_s.reshape(dec_batch, dec_seq, 2, B_KV_HEADS, 2 * B_HEAD_DIM))
            subln = row(b_subln[ib])
            width = 2 * B_HEAD_DIM
            k_slices = [slice(n * width, (n + 1) * width) for n in range(B_KV_HEADS)]
            v_slices = [slice(nk + n * width, nk + (n + 1) * width) for n in range(B_KV_HEADS)]
            vheads = [(slice(hh * width, (hh + 1) * width), c, hh // B_REP, hh // B_REP)
                      for hh in range(B_HEADS) for c in range(2)]
            o_p = _flash_call(f"b{ib}_attn_p", q_p, kv_p, kv_p, [lam, subln], batch, seq, k_slices, v_slices,
                              vheads, functools.partial(_b_finalize, lam_init), width, nq)
            cache = cache_b_kv.reshape(cache_b_kv.shape[0] * n_pool, PAGE_SIZE, 2 * nk)
            o_s = _b_sample_call(f"b{ib}_attn_s", q_s, kv_s, lam, subln, cache, pt_flat + ib * n_pool,
                                 dec_batch, dec_seq, pages_per_seq, lam_init)
            wo = b_w_o[ib].astype(BF16)
            xp = _out_call(f"b{ib}_out_p", [o_p], [], xp, wo, tm_p)
            xs = _out_call(f"b{ib}_out_s", [o_s], [], xs, wo, tm_s)
            ib += 1
        else:
            c_down = C_Q_LORA + C_KV_LORA + C_ROPE
            wd = jnp.pad(c_w_down[ic], ((0, 0), (0, 128 - C_ROPE))).astype(BF16)
            wuq = c_w_uq[ic].reshape(C_Q_LORA, C_HEADS, C_NOPE + C_ROPE)
            wuq = jnp.pad(wuq, ((0, 0), (0, 0), (0, C_QK_PAD - C_NOPE - C_ROPE)))
            wuq = wuq.reshape(C_Q_LORA, C_HEADS * C_QK_PAD).astype(BF16)
            wukv = c_w_ukv[ic].reshape(C_KV_LORA, C_HEADS, C_NOPE + C_V)
            wuk = wukv[..., :C_NOPE].reshape(C_KV_LORA, C_HEADS * C_NOPE).astype(BF16)
            wuv = wukv[..., C_NOPE:].reshape(C_KV_LORA, C_HEADS * C_V).astype(BF16)
            qn, kn = c_q_norm[ic], c_k_norm[ic]
            gq = row(jnp.concatenate([qn, jnp.zeros((C_QK_PAD - C_NOPE - C_ROPE,), F32)]) * C_SCALE)
            kpg = row(jnp.concatenate([kn[C_NOPE:], jnp.zeros((128 - C_ROPE,), F32)]))
            gk = row(kn[:C_NOPE])
            cproj = lambda nm, x, tab, tm, qdt: _c_proj_call(
                nm, x, g_attn, wd, row(c_q_lora_norm[ic]), row(c_kv_norm[ic]), kpg, tab['ck'][0], tab['ck'][1],
                wuq, bmat_cq, gq, tab['cq'][0], tab['cq'][1], wuk, gk, wuv, qdt, tm)
            lat_p, q_p, k_p, v_p = cproj(f"c{ic}_proj_p", xp, tab_p, tm_p, BF16)
            lat_s, q_s, _, _ = cproj(f"c{ic}_proj_s", xs, tab_s, tm_s, F32)
            c_p.append(lat_p.reshape(batch, seq, C_LAT))
            c_s.append(lat_s.reshape(dec_batch, dec_seq, C_LAT))
            k_slices = [slice(hh * C_QK_PAD, (hh + 1) * C_QK_PAD) for hh in range(C_HEADS)]
            v_slices = [slice(hh * C_V, (hh + 1) * C_V) for hh in range(C_HEADS)]
            vheads = [(k_slices[hh], None, hh, hh) for hh in range(C_HEADS)]
            o_p = _flash_call(f"c{ic}_attn_p", q_p, k_p, v_p, [], batch, seq, k_slices, v_slices, vheads,
                              _c_finalize, C_V, C_HEADS * C_V)
            cache = cache_c_latent.reshape(cache_c_latent.shape[0] * n_pool, PAGE_SIZE, C_LAT)
            o_s = _c_sample_call(f"c{ic}_attn_s", q_s, lat_s, gk, wuk.T, wuk, wuv, cache,
                                 pt_flat + ic * n_pool, dec_batch, dec_seq, pages_per_seq)
            wo = c_w_o[ic].astype(BF16)
            xp = _out_call(f"c{ic}_out_p", [o_p], [], xp, wo, tm_p)
            xs = _out_call(f"c{ic}_out_s", [o_s], [], xs, wo, tm_s)
            ic += 1

        g_ffn = row(norm_ffn[i])
        wg, wu, wdn = ffn_w_gate[i].astype(BF16), ffn_w_up[i].astype(BF16), ffn_w_down[i].astype(BF16)
        cw, cb = ffn_conv_w[i].astype(F32), row(ffn_conv_b[i])
        xp, st_p = _ffn_prompt_call(f"ffn{i}_p", xp, seq, g_ffn, wg, wu, cw, cb, wdn, tm_p)
        ctx = state_ffn_conv[i]
        e1 = jnp.pad(ctx[:, 1:2], ((0, 0), (0, dec_seq - 1), (0, 0))).reshape(ts, D_FF)
        e2 = jnp.pad(ctx, ((0, 0), (0, dec_seq - 2), (0, 0))).reshape(ts, D_FF)
        xs, gate_s = _ffn_sample_call(f"ffn{i}_s", xs, dec_seq, e1, e2, g_ffn, wg, wu, cw, cb, wdn,
                                      min(FFN_SAMPLE_TILE, ts))
        f_p.append(st_p[:, 8 - (CONV_W - 1):])
        f_s.append(gate_s.reshape(dec_batch, dec_seq, D_FF)[:, dec_seq - (CONV_W - 1):])

    return (xp.reshape(batch, seq, d), xs.reshape(dec_batch, dec_seq, d),
            jnp.stack(a_p[0]), jnp.stack(a_p[1]), jnp.stack(a_p[2]),
            jnp.stack(b_p), jnp.stack(c_p), jnp.stack(f_p),
            jnp.stack(a_s[0]), jnp.stack(a_s[1]), jnp.stack(a_s[2]),
            jnp.stack(b_s), jnp.stack(c_s), jnp.stack(f_s))
```

```python
import functools
import math

import numpy as np
import jax
import jax.numpy as jnp
from jax import lax
from jax.experimental import pallas as pl
from jax.experimental.pallas import tpu as pltpu

F32 = jnp.float32
BF16 = jnp.bfloat16

D_MODEL = 1024
PAGE_SIZE = 128
N_MIXERS = 3
ROPE_THETA = 10000.0
NORM_EPS = 1e-6

A_GROUPS = ((128, 1), (512, 4), (2048, 16))
A_HEADS = 8
A_HEAD_DIM = 64
A_SCALE = A_HEAD_DIM ** -0.5
A_BLOCK = 128
A_GW = A_HEADS * A_HEAD_DIM

B_HEADS = 8
B_KV_HEADS = 2
B_HEAD_DIM = 64
B_SCALE = B_HEAD_DIM ** -0.5
B_REP = B_HEADS // B_KV_HEADS
B_PAIR = 2 * B_HEAD_DIM

C_HEADS = 8
C_Q_LORA = 384
C_KV_LORA = 256
C_NOPE = 128
C_ROPE = 64
C_V = 128
C_SCALE = (C_NOPE + C_ROPE) ** -0.5
C_QK_PAD = 256
C_LAT = C_KV_LORA + C_ROPE

D_FF = 2816
CONV_W = 3
FFN_CHUNK = 256

ROPE_DIM = 64
ROPE_HALF = 32
LANE = 128
LANE_CHUNK = 256
TOKEN_TILE = 512
FFN_SAMPLE_TILE = 256
FLASH_TILE = 512
PAGES_PER_STEP = 16
C_KEY_SUBTILE = 512
NEG_INF = float("-inf")
VMEM_LIMIT = 52 * 1024 * 1024


def _dot(a, b):
    return jnp.dot(a, b, preferred_element_type=F32)


def _dot_nt(a, b):
    return lax.dot_general(a, b, (((1,), (1,)), ((), ())), preferred_element_type=F32)


def _rms_rows(x, g):
    ms = jnp.mean(x * x, axis=-1, keepdims=True)
    return x * lax.rsqrt(ms + NORM_EPS) * g


def _seg_mean_sq(y, bmat):
    y2 = y * y
    hi = y2.astype(BF16)
    lo = (y2 - hi.astype(F32)).astype(BF16)
    return _dot(hi, bmat) + _dot(lo, bmat)


def _lo_half_mask(shape):
    return (lax.broadcasted_iota(jnp.int32, shape, len(shape) - 1) & (ROPE_DIM - 1)) < ROPE_HALF


def _swap_halves(y, lo_mask):
    n = y.shape[-1]
    return jnp.where(lo_mask, pltpu.roll(y, n - ROPE_HALF, 1), pltpu.roll(y, ROPE_HALF, 1))


def _norm_rope(y, bmat, gain, cos, sin, lo_mask):
    yn = y * lax.rsqrt(_seg_mean_sq(y, bmat) + NORM_EPS) * gain
    return yn * cos + _swap_halves(yn, lo_mask) * sin


def _const_spec(shape):
    nd = len(shape)
    return pl.BlockSpec(shape, lambda *_: (0,) * nd, pipeline_mode=pl.Buffered(1))


def _params(n_axes):
    return pltpu.CompilerParams(dimension_semantics=("arbitrary",) * n_axes,
                                vmem_limit_bytes=VMEM_LIMIT)


def _own_block_mask(shape, rows_per_head, cols_per_head):
    row_head = lax.broadcasted_iota(jnp.int32, shape, 0) >> int(math.log2(rows_per_head))
    col_head = lax.broadcasted_iota(jnp.int32, shape, 1) >> int(math.log2(cols_per_head))
    return row_head == col_head


def _proj_kernel(plan, tiles_per_seq, n_out, x_ref, g_ref, w_ref, bmat_ref, gain_ref, cos_ref, sin_ref, *rest):
    o_refs = rest[:n_out]
    ybuf = rest[n_out] if len(rest) > n_out else None
    tm = x_ref.shape[0]
    h = _rms_rows(x_ref[...], g_ref[...]).astype(BF16)
    cos = cos_ref[...]
    sin = sin_ref[...]
    lo_mask = _lo_half_mask(cos.shape)
    tile_in_seq = pl.program_id(0) % tiles_per_seq
    for c, (mode, dests) in enumerate(plan):
        sl = slice(c * LANE_CHUNK, (c + 1) * LANE_CHUNK)
        y = _dot(h, w_ref[:, sl])
        if mode:
            y = _norm_rope(y, bmat_ref[...], gain_ref[:, sl], cos, sin, lo_mask)
        for dest in dests:
            kind, oi = dest[0], dest[1]
            o_ref = o_refs[oi]
            if kind == "rows":
                o_ref[:, dest[2]:dest[2] + LANE_CHUNK] = y.astype(o_ref.dtype)
            elif kind == "cols":
                o_ref[0, dest[2]:dest[2] + LANE_CHUNK, :] = y.T.astype(o_ref.dtype)
            elif kind == "cols_comp":
                yt = y.T
                first = lax.broadcasted_iota(jnp.int32, (B_PAIR, 1), 0) < B_HEAD_DIM
                for pair in range(LANE_CHUNK // B_PAIR):
                    blk = yt[pair * B_PAIR:(pair + 1) * B_PAIR]
                    base = 2 * dest[2] + pair * 2 * B_PAIR
                    o_ref[0, base:base + B_PAIR, :] = jnp.where(first, blk, 0.0).astype(o_ref.dtype)
                    o_ref[0, base + B_PAIR:base + 2 * B_PAIR, :] = jnp.where(first, 0.0, blk).astype(o_ref.dtype)
            elif kind == "stream":
                col, dil = dest[2], dest[3]
                if dil == 1:
                    o_ref[0, 0, :, col:col + LANE_CHUNK] = y
                else:
                    for part in range(LANE_CHUNK // LANE):
                        ybuf[part] = y[:, part * LANE:(part + 1) * LANE]
                    for r in range(dil):
                        for part in range(LANE_CHUNK // LANE):
                            o_ref[0, r, :, col + part * LANE:col + (part + 1) * LANE] = (
                                ybuf[part, pl.ds(r, tm // dil, stride=dil), :])
            else:
                kv, half, keep = dest[2], dest[3], dest[4]
                width = min(keep, tm)

                @pl.when(tile_in_seq >= tiles_per_seq - max(keep // tm, 1))
                def _(y=y, o_ref=o_ref, kv=kv, half=half, width=width):
                    o_ref[0, kv, half * LANE_CHUNK:(half + 1) * LANE_CHUNK, :] = y.T[:, tm - width:]


def _proj_call(name, x, g, w, bmat, gain, cos, sin, plan, outs, batch, seq, tm):
    t, d = x.shape
    n = w.shape[1]
    nrep = cos.shape[0] // tm
    tps = seq // tm
    out_specs, out_shapes = [], []
    need_buf = False
    for o in outs:
        kind, dt = o[0], o[1]
        if kind == "rows":
            out_specs.append(pl.BlockSpec((tm, o[2]), lambda i: (i, 0)))
            out_shapes.append(jax.ShapeDtypeStruct((t, o[2]), dt))
        elif kind == "cols":
            out_specs.append(pl.BlockSpec((1, o[2], tm), lambda i: (i // tps, 0, i % tps)))
            out_shapes.append(jax.ShapeDtypeStruct((batch, o[2], seq), dt))
        elif kind == "stream":
            dil = o[3]
            need_buf = need_buf or dil > 1
            out_specs.append(pl.BlockSpec((1, dil, tm // dil, o[2]), lambda i: (i // tps, 0, i % tps, 0)))
            out_shapes.append(jax.ShapeDtypeStruct((batch, dil, seq // dil, o[2]), dt))
        else:
            keep = o[2]
            first = tps - max(keep // tm, 1)
            out_specs.append(pl.BlockSpec((1, 2, A_GW, min(keep, tm)),
                                          lambda i, first=first: (i // tps, 0, 0, jnp.maximum(i % tps - first, 0))))
            out_shapes.append(jax.ShapeDtypeStruct((batch, 2, A_GW, keep), dt))
    scratch = [pltpu.VMEM((LANE_CHUNK // LANE, tm, LANE), F32)] if need_buf else []
    return pl.pallas_call(
        functools.partial(_proj_kernel, tuple(plan), tps, len(outs)),
        grid=(t // tm,),
        in_specs=[pl.BlockSpec((tm, d), lambda i: (i, 0)),
                  _const_spec((1, d)), _const_spec((d, n)), _const_spec(bmat.shape), _const_spec((1, n)),
                  pl.BlockSpec((tm, LANE_CHUNK), lambda i: (i % nrep, 0)),
                  pl.BlockSpec((tm, LANE_CHUNK), lambda i: (i % nrep, 0))],
        out_specs=out_specs,
        out_shape=out_shapes,
        scratch_shapes=scratch,
        compiler_params=_params(1),
        name=name,
    )(x, g, w, bmat, gain, cos, sin)


def _c_proj_kernel(prompt, x_ref, g_ref, wd_ref, qln_ref, kvn_ref, kpg_ref, cosk_ref, sink_ref,
                   wuq_ref, bq_ref, gq_ref, cosq_ref, sinq_ref, wuk_ref, gk_ref, wuv_ref,
                   lat_ref, q_ref, *kv_refs):
    h = _rms_rows(x_ref[...], g_ref[...]).astype(BF16)
    d = _dot(h, wd_ref[...])
    cq = _rms_rows(d[:, :C_Q_LORA], qln_ref[...])
    ckv = _rms_rows(d[:, C_Q_LORA:C_Q_LORA + C_KV_LORA], kvn_ref[...])
    kp = d[:, C_Q_LORA + C_KV_LORA:]
    ms = jnp.sum(kp * kp, axis=-1, keepdims=True) * (1.0 / C_ROPE)
    kpn = kp * lax.rsqrt(ms + NORM_EPS) * kpg_ref[...]
    kpe = kpn * cosk_ref[...] + _swap_halves(kpn, _lo_half_mask(kpn.shape)) * sink_ref[...]
    lat_ref[:, :C_KV_LORA] = ckv
    lat_ref[:, C_KV_LORA:] = kpe[:, :C_ROPE]
    cqb = cq.astype(BF16)
    cosq = cosq_ref[...]
    sinq = sinq_ref[...]
    lo_mask = _lo_half_mask(cosq.shape)
    for hh in range(C_HEADS):
        sq = slice(hh * C_QK_PAD, (hh + 1) * C_QK_PAD)
        q = _norm_rope(_dot(cqb, wuq_ref[:, sq]), bq_ref[...], gq_ref[...], cosq, sinq, lo_mask)
        if prompt:
            q_ref[0, sq, :] = q.T.astype(q_ref.dtype)
        else:
            q_ref[:, sq] = q
    if prompt:
        k_ref, vt_ref = kv_refs
        ckvb = ckv.astype(BF16)
        kpeb = kpe.astype(BF16)
        for hh in range(C_HEADS):
            yk = _dot(ckvb, wuk_ref[:, hh * C_NOPE:(hh + 1) * C_NOPE])
            k_ref[:, hh * C_QK_PAD:hh * C_QK_PAD + C_NOPE] = _rms_rows(yk, gk_ref[...]).astype(BF16)
            k_ref[:, hh * C_QK_PAD + C_NOPE:(hh + 1) * C_QK_PAD] = kpeb
        for c in range(C_HEADS * C_V // LANE_CHUNK):
            sl = slice(c * LANE_CHUNK, (c + 1) * LANE_CHUNK)
            vt_ref[0, sl, :] = _dot(ckvb, wuv_ref[:, sl]).T.astype(BF16)


def _c_proj_call(name, prompt, x, g, wd, qln, kvn, kpg, cosk, sink, wuq, bq, gq, cosq, sinq, wuk, gk, wuv,
                 batch, seq, tm):
    t, d = x.shape
    nrep = cosk.shape[0] // tm
    tps = seq // tm
    tab = lambda w: pl.BlockSpec((tm, w), lambda i: (i % nrep, 0))
    row = lambda w: pl.BlockSpec((tm, w), lambda i: (i, 0))
    col = lambda n: pl.BlockSpec((1, n, tm), lambda i: (i // tps, 0, i % tps))
    nq = C_HEADS * C_QK_PAD
    nv = C_HEADS * C_V
    if prompt:
        out_specs = [row(C_LAT), col(nq), row(nq), col(nv)]
        out_shape = [jax.ShapeDtypeStruct((t, C_LAT), F32), jax.ShapeDtypeStruct((batch, nq, seq), BF16),
                     jax.ShapeDtypeStruct((t, nq), BF16), jax.ShapeDtypeStruct((batch, nv, seq), BF16)]
    else:
        out_specs = [row(C_LAT), row(nq)]
        out_shape = [jax.ShapeDtypeStruct((t, C_LAT), F32), jax.ShapeDtypeStruct((t, nq), F32)]
    return pl.pallas_call(
        functools.partial(_c_proj_kernel, prompt),
        grid=(t // tm,),
        in_specs=[row(d), _const_spec((1, d)), _const_spec(wd.shape), _const_spec(qln.shape),
                  _const_spec(kvn.shape), _const_spec(kpg.shape), tab(LANE), tab(LANE),
                  _const_spec(wuq.shape), _const_spec(bq.shape), _const_spec(gq.shape),
                  tab(LANE_CHUNK), tab(LANE_CHUNK),
                  _const_spec(wuk.shape), _const_spec(gk.shape), _const_spec(wuv.shape)],
        out_specs=out_specs,
        out_shape=out_shape,
        compiler_params=_params(1),
        name=name,
    )(x, g, wd, qln, kvn, kpg, cosk, sink, wuq, bq, gq, cosq, sinq, wuk, gk, wuv)


def _out_kernel(a_ref, x_ref, w_ref, y_ref):
    y_ref[...] = x_ref[...] + _dot(a_ref[...].astype(BF16), w_ref[...])


def _out_call(name, a, x, w, tm):
    t, d = x.shape
    row = lambda n: pl.BlockSpec((tm, n), lambda i: (i, 0))
    return pl.pallas_call(
        _out_kernel,
        grid=(t // tm,),
        in_specs=[row(w.shape[0]), row(d), _const_spec(w.shape)],
        out_specs=row(d),
        out_shape=jax.ShapeDtypeStruct((t, d), F32),
        compiler_params=_params(1),
        name=name,
    )(a, x, w)


def _a_out_kernel(dils, *refs):
    ng = len(dils)
    o_refs, l_refs = refs[:ng], refs[ng:2 * ng]
    x_ref, w_ref, y_ref, obuf, lbuf = refs[2 * ng:]
    tm = x_ref.shape[0]
    nparts = A_GW // LANE
    outs, lses = [], []
    for g, dil in enumerate(dils):
        if dil == 1:
            outs.append(o_refs[g][0, 0])
            lses.append(l_refs[g][0, 0])
            continue
        for src, buf in ((o_refs[g], obuf), (l_refs[g], lbuf)):
            for r in range(dil):
                for part in range(nparts):
                    buf[g, part, pl.ds(r, tm // dil, stride=dil), :] = src[0, r, :, part * LANE:(part + 1) * LANE]
        outs.append(jnp.concatenate([obuf[g, part] for part in range(nparts)], axis=1))
        lses.append(jnp.concatenate([lbuf[g, part] for part in range(nparts)], axis=1))
    mx = functools.reduce(jnp.maximum, lses)
    es = [jnp.exp(l - mx) for l in lses]
    den = functools.reduce(lambda a, b: a + b, es)
    num = functools.reduce(lambda a, b: a + b, [e * o for e, o in zip(es, outs)])
    y_ref[...] = x_ref[...] + _dot((num / den).astype(BF16), w_ref[...])


def _a_out_call(name, outs, lses, dils, x, w, seq, tm):
    t, d = x.shape
    tps = seq // tm
    stream = lambda dil: pl.BlockSpec((1, dil, tm // dil, A_GW), lambda i: (i // tps, 0, i % tps, 0))
    row = lambda n: pl.BlockSpec((tm, n), lambda i: (i, 0))
    ng = len(dils)
    buf = pltpu.VMEM((ng, A_GW // LANE, tm, LANE), F32)
    return pl.pallas_call(
        functools.partial(_a_out_kernel, tuple(dils)),
        grid=(t // tm,),
        in_specs=[stream(dil) for dil in dils] * 2 + [row(d), _const_spec(w.shape)],
        out_specs=row(d),
        out_shape=jax.ShapeDtypeStruct((t, d), F32),
        scratch_shapes=[buf, buf],
        compiler_params=_params(1),
        name=name,
    )(*outs, *lses, x, w)


def _ffn_chunks(h, x, prev_fn, wg_ref, wu_ref, cw_ref, cb_ref, wd_ref, o_ref, st_fn):
    o_ref[...] = x
    for c in range(D_FF // FFN_CHUNK):
        sl = slice(c * FFN_CHUNK, (c + 1) * FFN_CHUNK)
        gt = _dot(h, wg_ref[:, sl])
        up = _dot(h, wu_ref[:, sl])
        p1, p2 = prev_fn(gt, sl)
        gc = cb_ref[:, sl] + cw_ref[0:1, sl] * p2
        gc = gc + cw_ref[1:2, sl] * p1
        gc = gc + cw_ref[2:3, sl] * gt
        act = (gc * jax.nn.sigmoid(gc)) * up
        o_ref[...] += _dot(act.astype(BF16), wd_ref[sl, :])
        st_fn(gt, sl)


def _ffn_prompt_kernel(tiles_per_seq, x_ref, xh_ref, g_ref, wg_ref, wu_ref, cw_ref, cb_ref, wd_ref,
                       o_ref, st_ref):
    x = x_ref[...]
    tm = x.shape[0]
    g = g_ref[...]
    h = _rms_rows(x, g).astype(BF16)
    hh = _rms_rows(xh_ref[...], g).astype(BF16)
    keep = (pl.program_id(0) % tiles_per_seq != 0).astype(F32)

    def prev_fn(gt, sl):
        gh = _dot(hh, wg_ref[:, sl]) * keep
        full = jnp.concatenate([gh, gt], axis=0)
        return pltpu.roll(full, 1, 0)[8:], pltpu.roll(full, 2, 0)[8:]

    def st_fn(gt, sl):
        st_ref[0, :, sl] = gt[tm - 8:]

    _ffn_chunks(h, x, prev_fn, wg_ref, wu_ref, cw_ref, cb_ref, wd_ref, o_ref, st_fn)


def _ffn_sample_kernel(seq, x_ref, e1_ref, e2_ref, g_ref, wg_ref, wu_ref, cw_ref, cb_ref, wd_ref,
                       o_ref, st_ref):
    x = x_ref[...]
    tm = x.shape[0]
    h = _rms_rows(x, g_ref[...]).astype(BF16)
    tpos = lax.broadcasted_iota(jnp.int32, (tm, 1), 0) & (seq - 1)

    def prev_fn(gt, sl):
        p1 = jnp.where(tpos >= 1, pltpu.roll(gt, 1, 0), e1_ref[:, sl])
        p2 = jnp.where(tpos >= 2, pltpu.roll(gt, 2, 0), e2_ref[:, sl])
        return p1, p2

    def st_fn(gt, sl):
        st_ref[:, sl] = gt

    _ffn_chunks(h, x, prev_fn, wg_ref, wu_ref, cw_ref, cb_ref, wd_ref, o_ref, st_fn)


def _ffn_weight_specs(wg, wu, cw, cb, wd):
    return [_const_spec(wg.shape), _const_spec(wu.shape), _const_spec(cw.shape), _const_spec(cb.shape),
            _const_spec(wd.shape)]


def _ffn_prompt_call(name, x, seq_len, g, wg, wu, cw, cb, wd, tm):
    t, d = x.shape
    tps = seq_len // tm
    hb = tm // 8
    return pl.pallas_call(
        functools.partial(_ffn_prompt_kernel, tps),
        grid=(t // tm,),
        in_specs=[pl.BlockSpec((tm, d), lambda i: (i, 0)),
                  pl.BlockSpec((8, d), lambda i: (jnp.maximum(i * hb - 1, 0), 0)),
                  _const_spec((1, d))] + _ffn_weight_specs(wg, wu, cw, cb, wd),
        out_specs=[pl.BlockSpec((tm, d), lambda i: (i, 0)),
                   pl.BlockSpec((1, 8, D_FF), lambda i: (i // tps, 0, 0))],
        out_shape=[jax.ShapeDtypeStruct((t, d), F32),
                   jax.ShapeDtypeStruct((t // seq_len, 8, D_FF), F32)],
        compiler_params=_params(1),
        name=name,
    )(x, x, g, wg, wu, cw, cb, wd)


def _ffn_sample_call(name, x, seq, e1, e2, g, wg, wu, cw, cb, wd, tm):
    t, d = x.shape
    row = lambda n: pl.BlockSpec((tm, n), lambda i: (i, 0))
    return pl.pallas_call(
        functools.partial(_ffn_sample_kernel, seq),
        grid=(t // tm,),
        in_specs=[row(d), row(D_FF), row(D_FF), _const_spec((1, d))] + _ffn_weight_specs(wg, wu, cw, cb, wd),
        out_specs=[row(d), row(D_FF)],
        out_shape=[jax.ShapeDtypeStruct((t, d), F32), jax.ShapeDtypeStruct((t, D_FF), F32)],
        compiler_params=_params(1),
        name=name,
    )(x, e1, e2, g, wg, wu, cw, cb, wd)


def _a_prompt_kernel(q_ref, kp_ref, kc_ref, vp_ref, vc_ref, o_ref, lse_ref):
    nblk = pl.program_id(2)
    q = q_ref[0, 0].astype(BF16)
    kk = jnp.concatenate([kp_ref[0, 0], kc_ref[0, 0]], axis=0).astype(BF16)
    vv = jnp.concatenate([vp_ref[0, 0], vc_ref[0, 0]], axis=0).astype(BF16)
    qi = lax.broadcasted_iota(jnp.int32, (A_BLOCK, 2 * A_BLOCK), 0) + A_BLOCK
    ki = lax.broadcasted_iota(jnp.int32, (A_BLOCK, 2 * A_BLOCK), 1)
    dlt = qi - ki
    first_key = jnp.where(nblk > 0, 0, A_BLOCK)
    mask = (dlt >= 0) & (dlt <= A_BLOCK) & (ki >= first_key)
    lo_lane = lax.broadcasted_iota(jnp.int32, (1, 2 * A_HEAD_DIM), 1) < A_HEAD_DIM
    for p in range(A_HEADS // 2):
        sl = slice(p * 2 * A_HEAD_DIM, (p + 1) * 2 * A_HEAD_DIM)
        qp, kpair, vpair = q[:, sl], kk[:, sl], vv[:, sl]
        o_pair = jnp.zeros((A_BLOCK, 2 * A_HEAD_DIM), F32)
        lse_pair = jnp.zeros((A_BLOCK, 2 * A_HEAD_DIM), F32)
        for half in range(2):
            hm = lo_lane if half == 0 else jnp.logical_not(lo_lane)
            s = _dot_nt(jnp.where(hm, qp, jnp.zeros_like(qp)), kpair)
            s = jnp.where(mask, s, NEG_INF)
            m = jnp.max(s, axis=-1, keepdims=True)
            e = jnp.exp(s - m)
            l = jnp.sum(e, axis=-1, keepdims=True)
            pv = _dot(e.astype(BF16), jnp.where(hm, vpair, jnp.zeros_like(vpair)))
            o_pair = o_pair + pv / l
            lse_pair = jnp.where(hm, m + jnp.log(l), lse_pair)
        o_ref[0, 0, :, sl] = o_pair
        lse_ref[0, 0, :, sl] = lse_pair


def _a_prompt_call(name, stream):
    batch, dil, m_len, _ = stream.shape
    nb = m_len // A_BLOCK
    blk = (1, 1, A_BLOCK, A_GW)
    cur = lambda off: pl.BlockSpec(blk, lambda b, r, n: (b, r, n, off))
    prev = lambda off: pl.BlockSpec(blk, lambda b, r, n: (b, r, jnp.maximum(n - 1, 0), off))
    o_spec = pl.BlockSpec(blk, lambda b, r, n: (b, r, n, 0))
    return pl.pallas_call(
        _a_prompt_kernel,
        grid=(batch, dil, nb),
        in_specs=[cur(0), prev(1), cur(1), prev(2), cur(2)],
        out_specs=[o_spec, o_spec],
        out_shape=[jax.ShapeDtypeStruct((batch, dil, m_len, A_GW), F32)] * 2,
        compiler_params=_params(3),
        name=name,
    )(stream, stream, stream, stream, stream)


def _a_sample_kernel(seq, qkv_ref, c0_ref, c1_ref, c2_ref, o_ref):
    rows = A_HEADS * seq
    qkv = qkv_ref[0]
    blockmask = _own_block_mask((rows, A_GW), seq, A_HEAD_DIM)
    outs, lses = [], []
    for g, ((window, dil), c_ref) in enumerate(zip(A_GROUPS, (c0_ref, c1_ref, c2_ref))):
        base = g * 3 * A_GW
        q = qkv[:, base:base + A_GW]
        kn = qkv[:, base + A_GW:base + 2 * A_GW]
        vn = qkv[:, base + 2 * A_GW:base + 3 * A_GW]
        qblk = jnp.where(blockmask, jnp.concatenate([q] * A_HEADS, axis=0), 0.0).astype(BF16)
        buf_len = c_ref.shape[4]
        kct = c_ref[0, 0, 0].astype(BF16)
        vct = c_ref[0, 0, 1].astype(BF16)
        s = _dot(qblk, kct)
        kidx = lax.broadcasted_iota(jnp.int32, (rows, buf_len), 1)
        trow = lax.broadcasted_iota(jnp.int32, (rows, buf_len), 0) & (seq - 1)
        back = buf_len + trow - kidx
        valid = (back <= window) & ((back & (dil - 1)) == 0)
        s = jnp.where(valid, s, NEG_INF)
        zpad = jnp.zeros((A_BLOCK - seq, A_GW), F32)
        knp = jnp.concatenate([kn, zpad], axis=0).astype(BF16)
        vnp = jnp.concatenate([vn, zpad], axis=0).astype(BF16)
        s2 = _dot_nt(qblk, knp)
        u = lax.broadcasted_iota(jnp.int32, (rows, A_BLOCK), 1)
        t2 = lax.broadcasted_iota(jnp.int32, (rows, A_BLOCK), 0) & (seq - 1)
        valid2 = (u <= t2) & (((t2 - u) & (dil - 1)) == 0)
        s2 = jnp.where(valid2, s2, NEG_INF)
        m = jnp.maximum(jnp.max(s, axis=-1, keepdims=True), jnp.max(s2, axis=-1, keepdims=True))
        e = jnp.exp(s - m)
        e2 = jnp.exp(s2 - m)
        l = jnp.sum(e, axis=-1, keepdims=True) + jnp.sum(e2, axis=-1, keepdims=True)
        outs.append((_dot_nt(e.astype(BF16), vct) + _dot(e2.astype(BF16), vnp)) / l)
        lses.append(m + jnp.log(l))
    mx = functools.reduce(jnp.maximum, lses)
    es = [jnp.exp(l - mx) for l in lses]
    den = es[0] + es[1] + es[2]
    merged = (es[0] * outs[0] + es[1] * outs[1] + es[2] * outs[2]) / den
    merged = jnp.where(blockmask, merged, 0.0)
    o = merged[0:seq]
    for hh in range(1, A_HEADS):
        o = o + merged[hh * seq:(hh + 1) * seq]
    o_ref[0] = o


def _a_sample_call(name, qkv, caches, layer, dec_batch, seq):
    cols = qkv.shape[1]
    views = [jnp.transpose(c, (0, 1, 3, 4, 5, 2)).reshape(c.shape[0], c.shape[1], 2, A_GW, c.shape[2])
             for c in caches]
    cspec = lambda v: pl.BlockSpec((1, 1, 2, A_GW, v.shape[4]), lambda b: (layer, b, 0, 0, 0))
    return pl.pallas_call(
        functools.partial(_a_sample_kernel, seq),
        grid=(dec_batch,),
        in_specs=[pl.BlockSpec((1, seq, cols), lambda b: (b, 0, 0))] + [cspec(v) for v in views],
        out_specs=pl.BlockSpec((1, seq, A_GW), lambda b: (b, 0, 0)),
        out_shape=jax.ShapeDtypeStruct((dec_batch, seq, A_GW), F32),
        compiler_params=_params(1),
        name=name,
    )(qkv.reshape(dec_batch, seq, cols), *views)


def _flash_kernel(vheads, finalize, qt_ref, k_ref, vt_ref, *rest):
    extra = rest[:-4]
    o_ref, m_scr, l_scr, acc_scr = rest[-4:]
    i = pl.program_id(1)
    j = pl.program_id(2)
    tq = qt_ref.shape[2]
    tk = k_ref.shape[1]

    @pl.when(j == 0)
    def _():
        m_scr[...] = jnp.full(m_scr.shape, NEG_INF, F32)
        l_scr[...] = jnp.zeros(l_scr.shape, F32)
        acc_scr[...] = jnp.zeros(acc_scr.shape, F32)

    def step(masked):
        if masked:
            causal = (lax.broadcasted_iota(jnp.int32, (tk, tq), 0)
                      <= lax.broadcasted_iota(jnp.int32, (tk, tq), 1))
        for vh, (qs, ks, vs) in enumerate(vheads):
            st = _dot(k_ref[0, :, ks], qt_ref[0, qs, :])
            if masked:
                st = jnp.where(causal, st, NEG_INF)
            m_prev = m_scr[vh]
            m_new = jnp.maximum(m_prev, jnp.max(st, axis=0, keepdims=True))
            alpha = jnp.exp(m_prev - m_new)
            p = jnp.exp(st - m_new)
            l_scr[vh] = alpha * l_scr[vh] + jnp.sum(p, axis=0, keepdims=True)
            acc_scr[vh] = alpha * acc_scr[vh] + _dot(vt_ref[0, vs, :], p.astype(BF16))
            m_scr[vh] = m_new

    @pl.when(j < i)
    def _():
        step(False)

    @pl.when(j == i)
    def _():
        step(True)
        finalize(extra, o_ref, l_scr, acc_scr)


def _b_finalize(lam_init, extra, o_ref, l_scr, acc_scr):
    lam_ref, subln_ref = extra
    lam = lam_ref[...]
    for hh in range(B_HEADS):
        o1 = acc_scr[2 * hh] / l_scr[2 * hh]
        o2 = acc_scr[2 * hh + 1] / l_scr[2 * hh + 1]
        od = o1 - lam * o2
        ms = jnp.mean(od * od, axis=0, keepdims=True)
        od = (od * lax.rsqrt(ms + NORM_EPS) * subln_ref[...]) * (1.0 - lam_init)
        o_ref[0, :, hh * B_PAIR:(hh + 1) * B_PAIR] = od.T.astype(o_ref.dtype)


def _c_finalize(extra, o_ref, l_scr, acc_scr):
    for hh in range(C_HEADS):
        o_ref[0, :, hh * C_V:(hh + 1) * C_V] = (acc_scr[hh] / l_scr[hh]).T.astype(o_ref.dtype)


def _flash_call(name, qt, k, vt, extra, vheads, finalize, dv, n_out):
    batch, dq, seq = qt.shape
    tq = min(FLASH_TILE, seq)
    nq = seq // tq
    kv = k.reshape(batch, seq, k.shape[1])
    nvh = len(vheads)
    out = pl.pallas_call(
        functools.partial(_flash_kernel, tuple(vheads), finalize),
        grid=(batch, nq, nq),
        in_specs=[pl.BlockSpec((1, dq, tq), lambda b, i, j: (b, 0, i)),
                  pl.BlockSpec((1, tq, kv.shape[2]), lambda b, i, j: (b, jnp.minimum(i, j), 0)),
                  pl.BlockSpec((1, vt.shape[1], tq), lambda b, i, j: (b, 0, jnp.minimum(i, j)))]
                 + [_const_spec(e.shape) for e in extra],
        out_specs=pl.BlockSpec((1, tq, n_out), lambda b, i, j: (b, i, 0)),
        out_shape=jax.ShapeDtypeStruct((batch, seq, n_out), BF16),
        scratch_shapes=[pltpu.VMEM((nvh, 1, tq), F32), pltpu.VMEM((nvh, 1, tq), F32),
                        pltpu.VMEM((nvh, dv, tq), F32)],
        compiler_params=_params(3),
        name=name,
    )(qt, kv, vt, *extra)
    return out.reshape(batch * seq, n_out)


def _online_update(s, m_scr, l_scr):
    m_prev = m_scr[...]
    m_new = jnp.maximum(m_prev, jnp.max(s, axis=-1, keepdims=True))
    alpha = jnp.exp(m_prev - m_new)
    p = jnp.exp(s - m_new)
    l_scr[...] = alpha * l_scr[...] + jnp.sum(p, axis=-1, keepdims=True)
    m_scr[...] = m_new
    return alpha, p.astype(BF16)


def _new_row_mask(rows, seq):
    u = lax.broadcasted_iota(jnp.int32, (rows, PAGE_SIZE), 1)
    t = lax.broadcasted_iota(jnp.int32, (rows, PAGE_SIZE), 0) & (seq - 1)
    return u <= t


def _b_sample_kernel(n_pages, seq, lam_init, pt_ref, q_ref, kvn_ref, lam_ref, subln_ref, *rest):
    page_refs = rest[:n_pages]
    o_ref, qall_scr, kv_scr, m_scr, l_scr, acc_scr = rest[n_pages:]
    j = pl.program_id(1)
    rows = B_KV_HEADS * 2 * B_REP * seq
    half_rows = rows // B_KV_HEADS
    kw = B_KV_HEADS * B_PAIR
    nsec = 2 * B_KV_HEADS

    @pl.when(j == 0)
    def _():
        q = q_ref[0]
        lane = lax.broadcasted_iota(jnp.int32, (1, B_PAIR), 1)
        zero = jnp.zeros((seq, B_PAIR), F32)
        pieces = []
        for n in range(B_KV_HEADS):
            for c in range(2):
                keep = (lane < B_HEAD_DIM) if c == 0 else (lane >= B_HEAD_DIM)
                for g in range(B_REP):
                    hh = n * B_REP + g
                    qp = jnp.where(keep, q[:, hh * B_PAIR:(hh + 1) * B_PAIR], 0.0)
                    parts = [zero] * B_KV_HEADS
                    parts[n] = qp
                    pieces.append(jnp.concatenate(parts, axis=1))
        qall_scr[...] = jnp.concatenate(pieces, axis=0).astype(BF16)
        m_scr[...] = jnp.full(m_scr.shape, NEG_INF, F32)
        l_scr[...] = jnp.zeros(l_scr.shape, F32)
        acc_scr[...] = jnp.zeros(acc_scr.shape, F32)

    for p in range(n_pages):
        for sec in range(nsec):
            kv_scr[p * PAGE_SIZE:(p + 1) * PAGE_SIZE, sec * B_PAIR:(sec + 1) * B_PAIR] = (
                page_refs[p][0, pl.ds(sec, PAGE_SIZE, stride=nsec), :].astype(BF16))

    first_kv_head = lax.broadcasted_iota(jnp.int32, (rows, 1), 0) < half_rows

    def update(kt, vt, mask):
        s = _dot_nt(qall_scr[...], kt)
        if mask is not None:
            s = jnp.where(mask, s, NEG_INF)
        alpha, p = _online_update(s, m_scr, l_scr)
        r = _dot(p, vt)
        acc_scr[...] = alpha * acc_scr[...] + jnp.where(first_kv_head, r[:, :B_PAIR], r[:, B_PAIR:])

    update(kv_scr[:, :kw], kv_scr[:, kw:], None)

    @pl.when(j == pl.num_programs(1) - 1)
    def _():
        zpad = jnp.zeros((PAGE_SIZE - seq, 2 * kw), F32)
        new = jnp.concatenate([kvn_ref[0], zpad], axis=0).astype(BF16)
        update(new[:, :kw], new[:, kw:], _new_row_mask(rows, seq))
        o = acc_scr[...] / l_scr[...]
        lam = lam_ref[...]
        grp = B_REP * seq
        for n in range(B_KV_HEADS):
            o1 = o[n * 2 * grp:n * 2 * grp + grp]
            o2 = o[n * 2 * grp + grp:(n + 1) * 2 * grp]
            od = _rms_rows(o1 - lam * o2, subln_ref[...]) * (1.0 - lam_init)
            for g in range(B_REP):
                hh = n * B_REP + g
                o_ref[0, :, hh * B_PAIR:(hh + 1) * B_PAIR] = od[g * seq:(g + 1) * seq]


def _page_specs(n_pages, pages_per_seq, block):
    def spec(p):
        return pl.BlockSpec((1,) + block,
                            lambda b, j, pt: (pt[b * pages_per_seq + j * n_pages + p], 0, 0))
    return [spec(p) for p in range(n_pages)]


def _b_sample_call(name, q, kvn, lam, subln, cache, pt_flat, dec_batch, seq, pages_per_seq, lam_init):
    n_pages = min(PAGES_PER_STEP, pages_per_seq)
    width = 2 * B_KV_HEADS * B_PAIR
    rows = B_KV_HEADS * 2 * B_REP * seq
    per_seq = lambda n: pl.BlockSpec((1, seq, n), lambda b, j, pt: (b, 0, 0))
    grid_spec = pltpu.PrefetchScalarGridSpec(
        num_scalar_prefetch=1,
        grid=(dec_batch, pages_per_seq // n_pages),
        in_specs=[per_seq(q.shape[1]), per_seq(width),
                  pl.BlockSpec(lam.shape, lambda b, j, pt: (0, 0)),
                  pl.BlockSpec(subln.shape, lambda b, j, pt: (0, 0))]
                 + _page_specs(n_pages, pages_per_seq, cache.shape[1:]),
        out_specs=per_seq(B_HEADS * B_PAIR),
        scratch_shapes=[pltpu.VMEM((rows, B_KV_HEADS * B_PAIR), BF16),
                        pltpu.VMEM((n_pages * PAGE_SIZE, width), BF16),
                        pltpu.VMEM((rows, 1), F32), pltpu.VMEM((rows, 1), F32),
                        pltpu.VMEM((rows, B_PAIR), F32)])
    out = pl.pallas_call(
        functools.partial(_b_sample_kernel, n_pages, seq, lam_init),
        grid_spec=grid_spec,
        out_shape=jax.ShapeDtypeStruct((dec_batch, seq, B_HEADS * B_PAIR), F32),
        compiler_params=_params(2),
        name=name,
    )(pt_flat, q.reshape(dec_batch, seq, -1), kvn.reshape(dec_batch, seq, -1), lam, subln,
      *([cache] * n_pages))
    return out.reshape(dec_batch * seq, -1)


def _c_sample_kernel(n_pages, seq, pt_ref, q_ref, latn_ref, gk_ref, wukt_ref, wuk_ref, wuv_ref, *rest):
    page_refs = rest[:n_pages]
    o_ref, qp_scr, qr_scr, lat_scr, m_scr, l_scr, acc_scr = rest[n_pages:]
    j = pl.program_id(1)
    rows = C_HEADS * seq

    @pl.when(j == 0)
    def _():
        q = q_ref[0]
        zero = jnp.zeros((seq, C_NOPE), F32)
        blocks, ropes = [], []
        for hh in range(C_HEADS):
            parts = [zero] * C_HEADS
            parts[hh] = q[:, hh * C_QK_PAD:hh * C_QK_PAD + C_NOPE] * gk_ref[...]
            blocks.append(jnp.concatenate(parts, axis=1))
            ropes.append(q[:, hh * C_QK_PAD + C_NOPE:hh * C_QK_PAD + C_NOPE + C_ROPE])
        qn = jnp.concatenate(blocks, axis=0).astype(BF16)
        qp_scr[...] = _dot_nt(qn, wuk_ref[...]).astype(BF16)
        qr_scr[...] = jnp.concatenate(ropes, axis=0).astype(BF16)
        m_scr[...] = jnp.full(m_scr.shape, NEG_INF, F32)
        l_scr[...] = jnp.zeros(l_scr.shape, F32)
        acc_scr[...] = jnp.zeros(acc_scr.shape, F32)

    for p in range(n_pages):
        lat_scr[:, p * PAGE_SIZE:(p + 1) * PAGE_SIZE] = page_refs[p][0].astype(BF16)

    def tile(lt, mask):
        ckvt = lt[:C_KV_LORA]
        kpet = lt[C_KV_LORA:]
        nk = lt.shape[1]
        kxt = _dot(wukt_ref[...], ckvt)
        kx2 = kxt * kxt
        scale_rows = []
        for hh in range(C_HEADS):
            ms = jnp.sum(kx2[hh * C_NOPE:(hh + 1) * C_NOPE], axis=0, keepdims=True) * (1.0 / C_NOPE)
            scale_rows.append(jnp.broadcast_to(lax.rsqrt(ms + NORM_EPS), (seq, nk)))
        s = _dot(qp_scr[...], ckvt) * jnp.concatenate(scale_rows, axis=0) + _dot(qr_scr[...], kpet)
        if mask is not None:
            s = jnp.where(mask, s, NEG_INF)
        alpha, p = _online_update(s, m_scr, l_scr)
        acc_scr[...] = alpha * acc_scr[...] + _dot_nt(p, ckvt)

    sub = min(C_KEY_SUBTILE, n_pages * PAGE_SIZE)
    for t in range(n_pages * PAGE_SIZE // sub):
        tile(lat_scr[:, t * sub:(t + 1) * sub], None)

    @pl.when(j == pl.num_programs(1) - 1)
    def _():
        tile(latn_ref[0].astype(BF16), _new_row_mask(rows, seq))
        o_lat = (acc_scr[...] / l_scr[...]).astype(BF16)
        r = _dot(o_lat, wuv_ref[...])
        r = jnp.where(_own_block_mask((rows, C_HEADS * C_V), seq, C_V), r, 0.0)
        o = r[0:seq]
        for hh in range(1, C_HEADS):
            o = o + r[hh * seq:(hh + 1) * seq]
        o_ref[0] = o


def _c_sample_call(name, q, latn_t, gk, wukt, wuk, wuv, cache, pt_flat, dec_batch, seq, pages_per_seq):
    n_pages = min(PAGES_PER_STEP, pages_per_seq)
    rows = C_HEADS * seq
    per_seq = lambda n: pl.BlockSpec((1, seq, n), lambda b, j, pt: (b, 0, 0))
    const = lambda a: pl.BlockSpec(a.shape, lambda b, j, pt: (0, 0))
    grid_spec = pltpu.PrefetchScalarGridSpec(
        num_scalar_prefetch=1,
        grid=(dec_batch, pages_per_seq // n_pages),
        in_specs=[per_seq(q.shape[1]),
                  pl.BlockSpec((1, C_LAT, PAGE_SIZE), lambda b, j, pt: (b, 0, 0)),
                  const(gk), const(wukt), const(wuk), const(wuv)]
                 + _page_specs(n_pages, pages_per_seq, cache.shape[1:]),
        out_specs=per_seq(C_HEADS * C_V),
        scratch_shapes=[pltpu.VMEM((rows, C_KV_LORA), BF16), pltpu.VMEM((rows, C_ROPE), BF16),
                        pltpu.VMEM((C_LAT, n_pages * PAGE_SIZE), BF16),
                        pltpu.VMEM((rows, 1), F32), pltpu.VMEM((rows, 1), F32),
                        pltpu.VMEM((rows, C_KV_LORA), F32)])
    out = pl.pallas_call(
        functools.partial(_c_sample_kernel, n_pages, seq),
        grid_spec=grid_spec,
        out_shape=jax.ShapeDtypeStruct((dec_batch, seq, C_HEADS * C_V), F32),
        compiler_params=_params(2),
        name=name,
    )(pt_flat, q.reshape(dec_batch, seq, -1), latn_t, gk, wukt, wuk, wuv, *([cache] * n_pages))
    return out.reshape(dec_batch * seq, -1)


def _block_mean_matrix(widths):
    m = np.zeros((LANE_CHUNK, LANE_CHUNK), np.float32)
    o = 0
    for w in widths:
        m[o:o + w, o:o + w] = 1.0 / w
        o += w
    return jnp.asarray(m, BF16)


def _rope_tables(pos):
    inv = ROPE_THETA ** (-jnp.arange(ROPE_HALF, dtype=F32) / ROPE_HALF)
    ang = pos.astype(F32)[:, None] * inv[None, :]
    cos, sin = jnp.cos(ang), jnp.sin(ang)
    return jnp.concatenate([cos, cos], axis=1), jnp.concatenate([-sin, sin], axis=1)


def _tables(pos):
    cos64, sin64 = _rope_tables(pos)
    n = pos.shape[0]
    one, zero = jnp.ones((n, ROPE_DIM), F32), jnp.zeros((n, ROPE_DIM), F32)
    return dict(
        full=(jnp.tile(cos64, (1, 4)), jnp.tile(sin64, (1, 4))),
        cq=(jnp.concatenate([one, one, cos64, one], 1), jnp.concatenate([zero, zero, sin64, zero], 1)),
        ck=(jnp.concatenate([cos64, one], 1), jnp.concatenate([sin64, zero], 1)))


def kernel(x_prompt, x_sample, cache_a_g0, cache_a_g1, cache_a_g2, cache_b_kv, cache_c_latent,
           state_ffn_conv, page_table, norm_attn, norm_ffn, a_w_qkv, a_q_norm, a_k_norm, a_w_o,
           b_w_qkv, b_q_norm, b_k_norm, b_lambda, b_subln, b_w_o, c_w_down, c_q_lora_norm, c_w_uq,
           c_kv_norm, c_w_ukv, c_q_norm, c_k_norm, c_w_o, ffn_w_gate, ffn_w_up, ffn_conv_w, ffn_conv_b,
           ffn_w_down):
    batch, seq, d = x_prompt.shape
    dec_batch, dec_seq, _ = x_sample.shape
    depth = norm_attn.shape[0]
    pages_per_seq = page_table.shape[1]
    past = pages_per_seq * PAGE_SIZE
    n_pool = cache_b_kv.shape[1] if cache_b_kv.shape[0] else cache_c_latent.shape[1]
    a_caches = (cache_a_g0, cache_a_g1, cache_a_g2)
    for (window, dil), c in zip(A_GROUPS, a_caches):
        assert c.shape[2] == window and seq % (dil * A_BLOCK) == 0
    assert dec_seq == 8 and A_BLOCK == PAGE_SIZE

    tp, ts = batch * seq, dec_batch * dec_seq
    tm_p, tm_s = min(TOKEN_TILE, seq), min(TOKEN_TILE, ts)
    xp = x_prompt.reshape(tp, d)
    xs = x_sample.reshape(ts, d)
    tab_p = _tables(jnp.arange(seq))
    tab_s = _tables(jnp.tile(past + jnp.arange(dec_seq), tm_s // dec_seq))
    pt_flat = page_table.reshape(-1)
    bmat64 = _block_mean_matrix((64, 64, 64, 64))
    bmat_cq = _block_mean_matrix((128, 64, 64))
    row = lambda v: v.reshape(1, -1).astype(F32)
    ng = len(A_GROUPS)
    dils = [dil for _, dil in A_GROUPS]

    a_p = [[] for _ in A_GROUPS]
    a_s = [[] for _ in A_GROUPS]
    b_p, b_s, c_p, c_s, f_p, f_s = [], [], [], [], [], []
    ia = ib = ic = 0
    for i in range(depth):
        g_attn = row(norm_attn[i])
        if i % N_MIXERS == 0:
            w = a_w_qkv[ia].astype(BF16)
            ncol = w.shape[1]
            gains = []
            for g in range(ng):
                gains += [jnp.tile(a_q_norm[ia, g], A_HEADS) * A_SCALE, jnp.tile(a_k_norm[ia, g], A_HEADS),
                          jnp.ones((A_GW,), F32)]
            gain = row(jnp.concatenate(gains))
            nch = ncol // LANE_CHUNK
            per_group = 3 * A_GW // LANE_CHUNK
            plan_p, plan_s = [], []
            for c in range(nch):
                g, cc = divmod(c, per_group)
                dests = [("stream", g, cc * LANE_CHUNK, dils[g])]
                if cc >= 2:
                    dests.append(("keep_t", ng + g, (cc - 2) // 2, cc % 2, min(A_GROUPS[g][0], seq)))
                plan_p.append((cc < 4, tuple(dests)))
                plan_s.append((cc < 4, (("rows", 0, c * LANE_CHUNK),)))
            outs_p = ([("stream", F32, 3 * A_GW, dil) for dil in dils]
                      + [("keep_t", F32, min(window, seq)) for window, _ in A_GROUPS])
            res = _proj_call(f"a{ia}_proj_p", xp, g_attn, w, bmat64, gain, *tab_p['full'], plan_p, outs_p,
                             batch, seq, tm_p)
            streams, keeps = res[:ng], res[ng:]
            qkv_s = _proj_call(f"a{ia}_proj_s", xs, g_attn, w, bmat64, gain, *tab_s['full'], plan_s,
                               [("rows", F32, ncol)], 1, ts, tm_s)[0]
            outs, lses = [], []
            for g in range(ng):
                o, l = _a_prompt_call(f"a{ia}_attn_p{g}", streams[g])
                outs.append(o)
                lses.append(l)
                keep = keeps[g].shape[3]
                a_p[g].append(jnp.transpose(keeps[g].reshape(batch, 2, A_HEADS, A_HEAD_DIM, keep), (0, 4, 1, 2, 3)))
                a_s[g].append(qkv_s.reshape(dec_batch, dec_seq, ng, 3, A_HEADS, A_HEAD_DIM)[:, :, g, 1:3])
            wo = a_w_o[ia].astype(BF16)
            xp = _a_out_call(f"a{ia}_out_p", outs, lses, dils, xp, wo, seq, tm_p)
            os_ = _a_sample_call(f"a{ia}_attn_s", qkv_s, a_caches, ia, dec_batch, dec_seq)
            xs = _out_call(f"a{ia}_out_s", os_.reshape(ts, A_GW), xs, wo, tm_s)
            ia += 1
        elif i % N_MIXERS == 1:
            lam_init = 0.8 - 0.6 * math.exp(-0.3 * i)
            lpf = b_lambda[ib].astype(F32)
            lam = (jnp.exp(jnp.sum(lpf[0] * lpf[1])) - jnp.exp(jnp.sum(lpf[2] * lpf[3])) + lam_init).reshape(1, 1)
            w = b_w_qkv[ib].astype(BF16)
            nq = B_HEADS * B_PAIR
            nk = B_KV_HEADS * B_PAIR
            gain = row(jnp.concatenate([jnp.tile(b_q_norm[ib], 2 * B_HEADS) * B_SCALE,
                                        jnp.tile(b_k_norm[ib], 2 * B_KV_HEADS), jnp.ones((nk,), F32)]))
            nqc = nq // LANE_CHUNK
            assert nk == LANE_CHUNK
            plan_p = ([(True, (("cols_comp", 0, c * LANE_CHUNK),)) for c in range(nqc)]
                      + [(True, (("rows", 1, 0), ("rows", 2, 0))), (False, (("rows", 1, nk), ("cols", 3, 0)))])
            outs_p = [("cols", BF16, 2 * nq), ("rows", F32, 2 * nk), ("rows", BF16, nk), ("cols", BF16, nk)]
            qt_p, kv_p, kb_p, vt_p = _proj_call(f"b{ib}_proj_p", xp, g_attn, w, bmat64, gain, *tab_p['full'],
                                                plan_p, outs_p, batch, seq, tm_p)
            plan_s = ([(True, (("rows", 0, c * LANE_CHUNK),)) for c in range(nqc)]
                      + [(True, (("rows", 1, 0),)), (False, (("rows", 1, nk),))])
            q_s, kv_s = _proj_call(f"b{ib}_proj_s", xs, g_attn, w, bmat64, gain, *tab_s['full'], plan_s,
                                   [("rows", F32, nq), ("rows", F32, 2 * nk)], 1, ts, tm_s)
            b_p.append(kv_p.reshape(batch, seq, 2, B_KV_HEADS, B_PAIR))
            b_s.append(kv_s.reshape(dec_batch, dec_seq, 2, B_KV_HEADS, B_PAIR))
            vheads = [(slice(vh * B_PAIR, (vh + 1) * B_PAIR),
                       slice((vh // (2 * B_REP)) * B_PAIR, (vh // (2 * B_REP) + 1) * B_PAIR),
                       slice((vh // (2 * B_REP)) * B_PAIR, (vh // (2 * B_REP) + 1) * B_PAIR))
                      for vh in range(2 * B_HEADS)]
            o_p = _flash_call(f"b{ib}_attn_p", qt_p, kb_p, vt_p, [lam, b_subln[ib].reshape(-1, 1).astype(F32)],
                              vheads, functools.partial(_b_finalize, lam_init), B_PAIR, nq)
            cache = cache_b_kv.reshape(cache_b_kv.shape[0] * n_pool, PAGE_SIZE * 2 * B_KV_HEADS, B_PAIR)
            o_s = _b_sample_call(f"b{ib}_attn_s", q_s, kv_s, lam, row(b_subln[ib]), cache, pt_flat + ib * n_pool,
                                 dec_batch, dec_seq, pages_per_seq, lam_init)
            wo = b_w_o[ib].astype(BF16)
            xp = _out_call(f"b{ib}_out_p", o_p, xp, wo, tm_p)
            xs = _out_call(f"b{ib}_out_s", o_s, xs, wo, tm_s)
            ib += 1
        else:
            wd = jnp.pad(c_w_down[ic], ((0, 0), (0, LANE - C_ROPE))).astype(BF16)
            wuq = c_w_uq[ic].reshape(C_Q_LORA, C_HEADS, C_NOPE + C_ROPE)
            wuq = jnp.pad(wuq, ((0, 0), (0, 0), (0, C_QK_PAD - C_NOPE - C_ROPE)))
            wuq = wuq.reshape(C_Q_LORA, C_HEADS * C_QK_PAD).astype(BF16)
            wukv = c_w_ukv[ic].reshape(C_KV_LORA, C_HEADS, C_NOPE + C_V)
            wuk = wukv[..., :C_NOPE].reshape(C_KV_LORA, C_HEADS * C_NOPE).astype(BF16)
            wuv = wukv[..., C_NOPE:].reshape(C_KV_LORA, C_HEADS * C_V).astype(BF16)
            qn, kn = c_q_norm[ic], c_k_norm[ic]
            gq = row(jnp.concatenate([qn, jnp.zeros((C_QK_PAD - C_NOPE - C_ROPE,), F32)]) * C_SCALE)
            kpg = row(jnp.concatenate([kn[C_NOPE:], jnp.zeros((LANE - C_ROPE,), F32)]))
            gk = row(kn[:C_NOPE])
            cproj = lambda nm, prompt, x, tab, nb, sq, tm: _c_proj_call(
                nm, prompt, x, g_attn, wd, row(c_q_lora_norm[ic]), row(c_kv_norm[ic]), kpg, tab['ck'][0],
                tab['ck'][1], wuq, bmat_cq, gq, tab['cq'][0], tab['cq'][1], wuk, gk, wuv, nb, sq, tm)
            lat_p, qt_p, k_p, vt_p = cproj(f"c{ic}_proj_p", True, xp, tab_p, batch, seq, tm_p)
            lat_s, q_s = cproj(f"c{ic}_proj_s", False, xs, tab_s, 1, ts, tm_s)
            c_p.append(lat_p.reshape(batch, seq, C_LAT))
            c_s.append(lat_s.reshape(dec_batch, dec_seq, C_LAT))
            vheads = [(slice(hh * C_QK_PAD, (hh + 1) * C_QK_PAD), slice(hh * C_QK_PAD, (hh + 1) * C_QK_PAD),
                       slice(hh * C_V, (hh + 1) * C_V)) for hh in range(C_HEADS)]
            o_p = _flash_call(f"c{ic}_attn_p", qt_p, k_p, vt_p, [], vheads, _c_finalize, C_V, C_HEADS * C_V)
            cache = jnp.swapaxes(cache_c_latent, 2, 3).reshape(cache_c_latent.shape[0] * n_pool, C_LAT, PAGE_SIZE)
            latn_t = jnp.pad(jnp.swapaxes(lat_s.reshape(dec_batch, dec_seq, C_LAT), 1, 2),
                             ((0, 0), (0, 0), (0, PAGE_SIZE - dec_seq)))
            o_s = _c_sample_call(f"c{ic}_attn_s", q_s, latn_t, gk, wuk.T, wuk, wuv, cache,
                                 pt_flat + ic * n_pool, dec_batch, dec_seq, pages_per_seq)
            wo = c_w_o[ic].astype(BF16)
            xp = _out_call(f"c{ic}_out_p", o_p, xp, wo, tm_p)
            xs = _out_call(f"c{ic}_out_s", o_s, xs, wo, tm_s)
            ic += 1

        g_ffn = row(norm_ffn[i])
        wg, wu, wdn = ffn_w_gate[i].astype(BF16), ffn_w_up[i].astype(BF16), ffn_w_down[i].astype(BF16)
        cw, cb = ffn_conv_w[i].astype(F32), row(ffn_conv_b[i])
        xp, st_p = _ffn_prompt_call(f"ffn{i}_p", xp, seq, g_ffn, wg, wu, cw, cb, wdn, tm_p)
        ctx = state_ffn_conv[i]
        e1 = jnp.pad(ctx[:, 1:2], ((0, 0), (0, dec_seq - 1), (0, 0))).reshape(ts, D_FF)
        e2 = jnp.pad(ctx, ((0, 0), (0, dec_seq - 2), (0, 0))).reshape(ts, D_FF)
        xs, gate_s = _ffn_sample_call(f"ffn{i}_s", xs, dec_seq, e1, e2, g_ffn, wg, wu, cw, cb, wdn,
                                      min(FFN_SAMPLE_TILE, ts))
        f_p.append(st_p[:, 8 - (CONV_W - 1):])
        f_s.append(gate_s.reshape(dec_batch, dec_seq, D_FF)[:, dec_seq - (CONV_W - 1):])

    return (xp.reshape(batch, seq, d), xs.reshape(dec_batch, dec_seq, d),
            jnp.stack(a_p[0]), jnp.stack(a_p[1]), jnp.stack(a_p[2]),
            jnp.stack(b_p), jnp.stack(c_p), jnp.stack(f_p),
            jnp.stack(a_s[0]), jnp.stack(a_s[1]), jnp.stack(a_s[2]),
            jnp.stack(b_s), jnp.stack(c_s), jnp.stack(f_s))
```

```python
import functools
import math

import numpy as np
import jax
import jax.numpy as jnp
from jax import lax
from jax.experimental import pallas as pl
from jax.experimental.pallas import tpu as pltpu

F32 = jnp.float32
BF16 = jnp.bfloat16

D_MODEL = 1024
PAGE_SIZE = 128
N_MIXERS = 3
ROPE_THETA = 10000.0
NORM_EPS = 1e-6

A_GROUPS = ((128, 1), (512, 4), (2048, 16))
A_HEADS = 8
A_HEAD_DIM = 64
A_SCALE = A_HEAD_DIM ** -0.5
A_BLOCK = 128
A_GW = A_HEADS * A_HEAD_DIM
A_BLOCKS_PER_STEP = 4

B_HEADS = 8
B_KV_HEADS = 2
B_HEAD_DIM = 64
B_SCALE = B_HEAD_DIM ** -0.5
B_REP = B_HEADS // B_KV_HEADS
B_PAIR = 2 * B_HEAD_DIM

C_HEADS = 8
C_Q_LORA = 384
C_KV_LORA = 256
C_NOPE = 128
C_ROPE = 64
C_V = 128
C_SCALE = (C_NOPE + C_ROPE) ** -0.5
C_QK_PAD = 256
C_LAT = C_KV_LORA + C_ROPE

D_FF = 2816
CONV_W = 3
FFN_CHUNK = 256

ROPE_DIM = 64
ROPE_HALF = 32
LANE = 128
LANE_CHUNK = 256
TOKEN_TILE = 512
PROJ_DOT_BATCH = 3
FFN_SAMPLE_TILE = 256
FLASH_TILE = 1024
LOG2_E = math.log2(math.e)
PAGES_PER_STEP = 32
C_KEY_SUBTILE = 512
NEG_INF = float("-inf")
VMEM_LIMIT = 52 * 1024 * 1024


def _dot(a, b):
    return jnp.dot(a, b, preferred_element_type=F32)


def _dot_nt(a, b):
    return lax.dot_general(a, b, (((1,), (1,)), ((), ())), preferred_element_type=F32)


def _rms_rows(x, g):
    ms = jnp.mean(x * x, axis=-1, keepdims=True)
    return x * lax.rsqrt(ms + NORM_EPS) * g


def _seg_mean_sq(y, bmat):
    y2 = y * y
    hi = y2.astype(BF16)
    lo = (y2 - hi.astype(F32)).astype(BF16)
    return _dot(hi, bmat) + _dot(lo, bmat)


def _lo_half_mask(shape):
    return (lax.broadcasted_iota(jnp.int32, shape, len(shape) - 1) & (ROPE_DIM - 1)) < ROPE_HALF


def _swap_halves(y, lo_mask):
    n = y.shape[-1]
    return jnp.where(lo_mask, pltpu.roll(y, n - ROPE_HALF, 1), pltpu.roll(y, ROPE_HALF, 1))


def _norm_rope(y, bmat, gain, cos, sin, lo_mask):
    yn = y * lax.rsqrt(_seg_mean_sq(y, bmat) + NORM_EPS) * gain
    return yn * cos + _swap_halves(yn, lo_mask) * sin


def _const_spec(shape):
    nd = len(shape)
    return pl.BlockSpec(shape, lambda *_: (0,) * nd, pipeline_mode=pl.Buffered(1))


def _params(n_axes):
    return pltpu.CompilerParams(dimension_semantics=("arbitrary",) * n_axes,
                                vmem_limit_bytes=VMEM_LIMIT)


def _own_block_mask(shape, rows_per_head, cols_per_head):
    row_head = lax.broadcasted_iota(jnp.int32, shape, 0) >> int(math.log2(rows_per_head))
    col_head = lax.broadcasted_iota(jnp.int32, shape, 1) >> int(math.log2(cols_per_head))
    return row_head == col_head


def _proj_kernel(plan, tiles_per_seq, n_out, x_ref, g_ref, w_ref, bmat_ref, gain_ref, cos_ref, sin_ref, *rest):
    o_refs = rest[:n_out]
    ybuf = rest[n_out] if len(rest) > n_out else None
    tm = x_ref.shape[0]
    h = _rms_rows(x_ref[...], g_ref[...]).astype(BF16)
    cos = cos_ref[...]
    sin = sin_ref[...]
    lo_mask = _lo_half_mask(cos.shape)
    tile_in_seq = pl.program_id(0) % tiles_per_seq
    ys = {}
    for c, (mode, dests) in enumerate(plan):
        if c % PROJ_DOT_BATCH == 0:
            ys = {cc: _dot(h, w_ref[:, cc * LANE_CHUNK:(cc + 1) * LANE_CHUNK])
                  for cc in range(c, min(c + PROJ_DOT_BATCH, len(plan)))}
        sl = slice(c * LANE_CHUNK, (c + 1) * LANE_CHUNK)
        y = ys[c]
        if mode:
            y = _norm_rope(y, bmat_ref[...], gain_ref[:, sl], cos, sin, lo_mask)
        for dest in dests:
            kind, oi = dest[0], dest[1]
            o_ref = o_refs[oi]
            if kind == "rows":
                o_ref[:, dest[2]:dest[2] + LANE_CHUNK] = y.astype(o_ref.dtype)
            elif kind == "cols":
                o_ref[0, dest[2]:dest[2] + LANE_CHUNK, :] = y.T.astype(o_ref.dtype)
            elif kind == "cols_comp":
                yt = y.T
                first = lax.broadcasted_iota(jnp.int32, (B_PAIR, 1), 0) < B_HEAD_DIM
                for pair in range(LANE_CHUNK // B_PAIR):
                    blk = yt[pair * B_PAIR:(pair + 1) * B_PAIR]
                    base = 2 * dest[2] + pair * 2 * B_PAIR
                    o_ref[0, base:base + B_PAIR, :] = jnp.where(first, blk, 0.0).astype(o_ref.dtype)
                    o_ref[0, base + B_PAIR:base + 2 * B_PAIR, :] = jnp.where(first, 0.0, blk).astype(o_ref.dtype)
            elif kind == "stream":
                col, dil = dest[2], dest[3]
                if dil == 1:
                    o_ref[0, 0, :, col:col + LANE_CHUNK] = y
                else:
                    for part in range(LANE_CHUNK // LANE):
                        ybuf[part] = y[:, part * LANE:(part + 1) * LANE]
                    for r in range(dil):
                        for part in range(LANE_CHUNK // LANE):
                            o_ref[0, r, :, col + part * LANE:col + (part + 1) * LANE] = (
                                ybuf[part, pl.ds(r, tm // dil, stride=dil), :])
            else:
                kv, half, keep = dest[2], dest[3], dest[4]
                width = min(keep, tm)

                @pl.when(tile_in_seq >= tiles_per_seq - max(keep // tm, 1))
                def _(y=y, o_ref=o_ref, kv=kv, half=half, width=width):
                    o_ref[0, kv, half * LANE_CHUNK:(half + 1) * LANE_CHUNK, :] = y.T[:, tm - width:]


def _proj_call(name, x, g, w, bmat, gain, cos, sin, plan, outs, batch, seq, tm):
    t, d = x.shape
    n = w.shape[1]
    nrep = cos.shape[0] // tm
    tps = seq // tm
    out_specs, out_shapes = [], []
    need_buf = False
    for o in outs:
        kind, dt = o[0], o[1]
        if kind == "rows":
            out_specs.append(pl.BlockSpec((tm, o[2]), lambda i: (i, 0)))
            out_shapes.append(jax.ShapeDtypeStruct((t, o[2]), dt))
        elif kind == "cols":
            out_specs.append(pl.BlockSpec((1, o[2], tm), lambda i: (i // tps, 0, i % tps)))
            out_shapes.append(jax.ShapeDtypeStruct((batch, o[2], seq), dt))
        elif kind == "stream":
            dil = o[3]
            need_buf = need_buf or dil > 1
            out_specs.append(pl.BlockSpec((1, dil, tm // dil, o[2]), lambda i: (i // tps, 0, i % tps, 0)))
            out_shapes.append(jax.ShapeDtypeStruct((batch, dil, seq // dil, o[2]), dt))
        else:
            keep = o[2]
            first = tps - max(keep // tm, 1)
            out_specs.append(pl.BlockSpec((1, 2, A_GW, min(keep, tm)),
                                          lambda i, first=first: (i // tps, 0, 0, jnp.maximum(i % tps - first, 0))))
            out_shapes.append(jax.ShapeDtypeStruct((batch, 2, A_GW, keep), dt))
    scratch = [pltpu.VMEM((LANE_CHUNK // LANE, tm, LANE), F32)] if need_buf else []
    return pl.pallas_call(
        functools.partial(_proj_kernel, tuple(plan), tps, len(outs)),
        grid=(t // tm,),
        in_specs=[pl.BlockSpec((tm, d), lambda i: (i, 0)),
                  _const_spec((1, d)), _const_spec((d, n)), _const_spec(bmat.shape), _const_spec((1, n)),
                  pl.BlockSpec((tm, LANE_CHUNK), lambda i: (i % nrep, 0)),
                  pl.BlockSpec((tm, LANE_CHUNK), lambda i: (i % nrep, 0))],
        out_specs=out_specs,
        out_shape=out_shapes,
        scratch_shapes=scratch,
        compiler_params=_params(1),
        name=name,
    )(x, g, w, bmat, gain, cos, sin)


def _c_proj_kernel(prompt, x_ref, g_ref, wd_ref, qln_ref, kvn_ref, kpg_ref, cosk_ref, sink_ref,
                   wuq_ref, bq_ref, gq_ref, cosq_ref, sinq_ref, wuk_ref, gk_ref, wuv_ref,
                   lat_ref, q_ref, *kv_refs):
    h = _rms_rows(x_ref[...], g_ref[...]).astype(BF16)
    d = _dot(h, wd_ref[...])
    cq = _rms_rows(d[:, :C_Q_LORA], qln_ref[...])
    ckv = _rms_rows(d[:, C_Q_LORA:C_Q_LORA + C_KV_LORA], kvn_ref[...])
    kp = d[:, C_Q_LORA + C_KV_LORA:]
    ms = jnp.sum(kp * kp, axis=-1, keepdims=True) * (1.0 / C_ROPE)
    kpn = kp * lax.rsqrt(ms + NORM_EPS) * kpg_ref[...]
    kpe = kpn * cosk_ref[...] + _swap_halves(kpn, _lo_half_mask(kpn.shape)) * sink_ref[...]
    lat_ref[:, :C_KV_LORA] = ckv
    lat_ref[:, C_KV_LORA:] = kpe[:, :C_ROPE]
    cqb = cq.astype(BF16)
    cosq = cosq_ref[...]
    sinq = sinq_ref[...]
    lo_mask = _lo_half_mask(cosq.shape)
    for hh in range(C_HEADS):
        sq = slice(hh * C_QK_PAD, (hh + 1) * C_QK_PAD)
        q = _norm_rope(_dot(cqb, wuq_ref[:, sq]), bq_ref[...], gq_ref[...], cosq, sinq, lo_mask)
        if prompt:
            q_ref[0, sq, :] = q.T.astype(q_ref.dtype)
        else:
            q_ref[:, sq] = q
    if prompt:
        k_ref, vt_ref = kv_refs
        ckvb = ckv.astype(BF16)
        kpeb = kpe.astype(BF16)
        for hh in range(C_HEADS):
            yk = _dot(ckvb, wuk_ref[:, hh * C_NOPE:(hh + 1) * C_NOPE])
            k_ref[:, hh * C_QK_PAD:hh * C_QK_PAD + C_NOPE] = _rms_rows(yk, gk_ref[...]).astype(BF16)
            k_ref[:, hh * C_QK_PAD + C_NOPE:(hh + 1) * C_QK_PAD] = kpeb
        for c in range(C_HEADS * C_V // LANE_CHUNK):
            sl = slice(c * LANE_CHUNK, (c + 1) * LANE_CHUNK)
            vt_ref[0, sl, :] = _dot(ckvb, wuv_ref[:, sl]).T.astype(BF16)


def _c_proj_call(name, prompt, x, g, wd, qln, kvn, kpg, cosk, sink, wuq, bq, gq, cosq, sinq, wuk, gk, wuv,
                 batch, seq, tm):
    t, d = x.shape
    nrep = cosk.shape[0] // tm
    tps = seq // tm
    tab = lambda w: pl.BlockSpec((tm, w), lambda i: (i % nrep, 0))
    row = lambda w: pl.BlockSpec((tm, w), lambda i: (i, 0))
    col = lambda n: pl.BlockSpec((1, n, tm), lambda i: (i // tps, 0, i % tps))
    nq = C_HEADS * C_QK_PAD
    nv = C_HEADS * C_V
    if prompt:
        out_specs = [row(C_LAT), col(nq), row(nq), col(nv)]
        out_shape = [jax.ShapeDtypeStruct((t, C_LAT), F32), jax.ShapeDtypeStruct((batch, nq, seq), BF16),
                     jax.ShapeDtypeStruct((t, nq), BF16), jax.ShapeDtypeStruct((batch, nv, seq), BF16)]
    else:
        out_specs = [row(C_LAT), row(nq)]
        out_shape = [jax.ShapeDtypeStruct((t, C_LAT), F32), jax.ShapeDtypeStruct((t, nq), F32)]
    return pl.pallas_call(
        functools.partial(_c_proj_kernel, prompt),
        grid=(t // tm,),
        in_specs=[row(d), _const_spec((1, d)), _const_spec(wd.shape), _const_spec(qln.shape),
                  _const_spec(kvn.shape), _const_spec(kpg.shape), tab(LANE), tab(LANE),
                  _const_spec(wuq.shape), _const_spec(bq.shape), _const_spec(gq.shape),
                  tab(LANE_CHUNK), tab(LANE_CHUNK),
                  _const_spec(wuk.shape), _const_spec(gk.shape), _const_spec(wuv.shape)],
        out_specs=out_specs,
        out_shape=out_shape,
        compiler_params=_params(1),
        name=name,
    )(x, g, wd, qln, kvn, kpg, cosk, sink, wuq, bq, gq, cosq, sinq, wuk, gk, wuv)


def _out_kernel(a_ref, x_ref, w_ref, y_ref):
    y_ref[...] = x_ref[...] + _dot(a_ref[...].astype(BF16), w_ref[...])


def _out_call(name, a, x, w, tm):
    t, d = x.shape
    row = lambda n: pl.BlockSpec((tm, n), lambda i: (i, 0))
    return pl.pallas_call(
        _out_kernel,
        grid=(t // tm,),
        in_specs=[row(w.shape[0]), row(d), _const_spec(w.shape)],
        out_specs=row(d),
        out_shape=jax.ShapeDtypeStruct((t, d), F32),
        compiler_params=_params(1),
        name=name,
    )(a, x, w)


def _a_out_kernel(dils, *refs):
    ng = len(dils)
    o_refs, l_refs = refs[:ng], refs[ng:2 * ng]
    x_ref, w_ref, y_ref, obuf, lbuf = refs[2 * ng:]
    tm = x_ref.shape[0]
    nparts = A_GW // LANE
    outs, lses = [], []
    for g, dil in enumerate(dils):
        if dil == 1:
            outs.append(o_refs[g][0, 0])
            lses.append(l_refs[g][0, 0])
            continue
        for src, buf in ((o_refs[g], obuf), (l_refs[g], lbuf)):
            for r in range(dil):
                for part in range(nparts):
                    buf[g, part, pl.ds(r, tm // dil, stride=dil), :] = src[0, r, :, part * LANE:(part + 1) * LANE]
        outs.append(jnp.concatenate([obuf[g, part] for part in range(nparts)], axis=1))
        lses.append(jnp.concatenate([lbuf[g, part] for part in range(nparts)], axis=1))
    mx = functools.reduce(jnp.maximum, lses)
    es = [jnp.exp(l - mx) for l in lses]
    den = functools.reduce(lambda a, b: a + b, es)
    num = functools.reduce(lambda a, b: a + b, [e * o for e, o in zip(es, outs)])
    y_ref[...] = x_ref[...] + _dot((num / den).astype(BF16), w_ref[...])


def _a_out_call(name, outs, lses, dils, x, w, seq, tm):
    t, d = x.shape
    tps = seq // tm
    stream = lambda dil: pl.BlockSpec((1, dil, tm // dil, A_GW), lambda i: (i // tps, 0, i % tps, 0))
    row = lambda n: pl.BlockSpec((tm, n), lambda i: (i, 0))
    ng = len(dils)
    buf = pltpu.VMEM((ng, A_GW // LANE, tm, LANE), F32)
    return pl.pallas_call(
        functools.partial(_a_out_kernel, tuple(dils)),
        grid=(t // tm,),
        in_specs=[stream(dil) for dil in dils] * 2 + [row(d), _const_spec(w.shape)],
        out_specs=row(d),
        out_shape=jax.ShapeDtypeStruct((t, d), F32),
        scratch_shapes=[buf, buf],
        compiler_params=_params(1),
        name=name,
    )(*outs, *lses, x, w)


def _ffn_chunks(h, x, prev_fn, wg_ref, wu_ref, cw_ref, cb_ref, wd_ref, o_ref, act_scr, st_fn):
    for c in range(D_FF // FFN_CHUNK):
        sl = slice(c * FFN_CHUNK, (c + 1) * FFN_CHUNK)
        gt = _dot(h, wg_ref[:, sl])
        up = _dot(h, wu_ref[:, sl])
        p1, p2 = prev_fn(gt, sl)
        gc = cb_ref[:, sl] + cw_ref[0:1, sl] * p2
        gc = gc + cw_ref[1:2, sl] * p1
        gc = gc + cw_ref[2:3, sl] * gt
        act_scr[:, sl] = ((gc * jax.nn.sigmoid(gc)) * up).astype(BF16)
        st_fn(gt, sl)
    o_ref[...] = x + _dot(act_scr[...], wd_ref[...])


def _ffn_prompt_kernel(tiles_per_seq, x_ref, xh_ref, g_ref, wg_ref, wu_ref, cw_ref, cb_ref, wd_ref,
                       o_ref, st_ref, act_scr):
    x = x_ref[...]
    tm = x.shape[0]
    g = g_ref[...]
    h = _rms_rows(x, g).astype(BF16)
    hh = _rms_rows(xh_ref[...], g).astype(BF16)
    keep = (pl.program_id(0) % tiles_per_seq != 0).astype(F32)

    def prev_fn(gt, sl):
        gh = _dot(hh, wg_ref[:, sl]) * keep
        full = jnp.concatenate([gh, gt], axis=0)
        return pltpu.roll(full, 1, 0)[8:], pltpu.roll(full, 2, 0)[8:]

    def st_fn(gt, sl):
        st_ref[0, :, sl] = gt[tm - 8:]

    _ffn_chunks(h, x, prev_fn, wg_ref, wu_ref, cw_ref, cb_ref, wd_ref, o_ref, act_scr, st_fn)


def _ffn_sample_kernel(seq, x_ref, e1_ref, e2_ref, g_ref, wg_ref, wu_ref, cw_ref, cb_ref, wd_ref,
                       o_ref, st_ref, act_scr):
    x = x_ref[...]
    tm = x.shape[0]
    h = _rms_rows(x, g_ref[...]).astype(BF16)
    tpos = lax.broadcasted_iota(jnp.int32, (tm, 1), 0) & (seq - 1)

    def prev_fn(gt, sl):
        p1 = jnp.where(tpos >= 1, pltpu.roll(gt, 1, 0), e1_ref[:, sl])
        p2 = jnp.where(tpos >= 2, pltpu.roll(gt, 2, 0), e2_ref[:, sl])
        return p1, p2

    def st_fn(gt, sl):
        st_ref[:, sl] = gt

    _ffn_chunks(h, x, prev_fn, wg_ref, wu_ref, cw_ref, cb_ref, wd_ref, o_ref, act_scr, st_fn)


def _ffn_weight_specs(wg, wu, cw, cb, wd):
    return [_const_spec(wg.shape), _const_spec(wu.shape), _const_spec(cw.shape), _const_spec(cb.shape),
            _const_spec(wd.shape)]


def _ffn_prompt_call(name, x, seq_len, g, wg, wu, cw, cb, wd, tm):
    t, d = x.shape
    tps = seq_len // tm
    hb = tm // 8
    return pl.pallas_call(
        functools.partial(_ffn_prompt_kernel, tps),
        grid=(t // tm,),
        in_specs=[pl.BlockSpec((tm, d), lambda i: (i, 0)),
                  pl.BlockSpec((8, d), lambda i: (jnp.maximum(i * hb - 1, 0), 0)),
                  _const_spec((1, d))] + _ffn_weight_specs(wg, wu, cw, cb, wd),
        out_specs=[pl.BlockSpec((tm, d), lambda i: (i, 0)),
                   pl.BlockSpec((1, 8, D_FF), lambda i: (i // tps, 0, 0))],
        out_shape=[jax.ShapeDtypeStruct((t, d), F32),
                   jax.ShapeDtypeStruct((t // seq_len, 8, D_FF), F32)],
        scratch_shapes=[pltpu.VMEM((tm, D_FF), BF16)],
        compiler_params=_params(1),
        name=name,
    )(x, x, g, wg, wu, cw, cb, wd)


def _ffn_sample_call(name, x, seq, e1, e2, g, wg, wu, cw, cb, wd, tm):
    t, d = x.shape
    row = lambda n: pl.BlockSpec((tm, n), lambda i: (i, 0))
    return pl.pallas_call(
        functools.partial(_ffn_sample_kernel, seq),
        grid=(t // tm,),
        in_specs=[row(d), row(D_FF), row(D_FF), _const_spec((1, d))] + _ffn_weight_specs(wg, wu, cw, cb, wd),
        out_specs=[row(d), row(D_FF)],
        out_shape=[jax.ShapeDtypeStruct((t, d), F32), jax.ShapeDtypeStruct((t, D_FF), F32)],
        scratch_shapes=[pltpu.VMEM((tm, D_FF), BF16)],
        compiler_params=_params(1),
        name=name,
    )(x, e1, e2, g, wg, wu, cw, cb, wd)


def _a_prompt_kernel(nsub, q_ref, kp_ref, kc_ref, vp_ref, vc_ref, o_ref, lse_ref):
    step = pl.program_id(2)
    q_all = q_ref[0, 0].astype(BF16)
    k_all = jnp.concatenate([kp_ref[0, 0], kc_ref[0, 0]], axis=0).astype(BF16)
    v_all = jnp.concatenate([vp_ref[0, 0], vc_ref[0, 0]], axis=0).astype(BF16)
    qi = lax.broadcasted_iota(jnp.int32, (A_BLOCK, 2 * A_BLOCK), 0) + A_BLOCK
    ki = lax.broadcasted_iota(jnp.int32, (A_BLOCK, 2 * A_BLOCK), 1)
    dlt = qi - ki
    band = (dlt >= 0) & (dlt <= A_BLOCK)
    first_key = jnp.where(step > 0, 0, A_BLOCK)
    lo_lane = lax.broadcasted_iota(jnp.int32, (1, 2 * A_HEAD_DIM), 1) < A_HEAD_DIM
    for sb in range(nsub):
        rows = slice(sb * A_BLOCK, (sb + 1) * A_BLOCK)
        keys = slice(sb * A_BLOCK, (sb + 2) * A_BLOCK)
        mask = (band & (ki >= first_key)) if sb == 0 else band
        for p in range(A_HEADS // 2):
            sl = slice(p * 2 * A_HEAD_DIM, (p + 1) * 2 * A_HEAD_DIM)
            qp, kpair, vpair = q_all[rows, sl], k_all[keys, sl], v_all[keys, sl]
            o_pair = jnp.zeros((A_BLOCK, 2 * A_HEAD_DIM), F32)
            lse_pair = jnp.zeros((A_BLOCK, 2 * A_HEAD_DIM), F32)
            for half in range(2):
                hm = lo_lane if half == 0 else jnp.logical_not(lo_lane)
                s = _dot_nt(jnp.where(hm, qp, jnp.zeros_like(qp)), kpair)
                s = jnp.where(mask, s, NEG_INF)
                m = jnp.max(s, axis=-1, keepdims=True)
                e = jnp.exp(s - m)
                l = jnp.sum(e, axis=-1, keepdims=True)
                pv = _dot(e.astype(BF16), jnp.where(hm, vpair, jnp.zeros_like(vpair)))
                o_pair = o_pair + pv / l
                lse_pair = jnp.where(hm, m + jnp.log(l), lse_pair)
            o_ref[0, 0, rows, sl] = o_pair
            lse_ref[0, 0, rows, sl] = lse_pair


def _a_prompt_call(name, stream):
    batch, dil, m_len, _ = stream.shape
    nb = m_len // A_BLOCK
    nsub = min(A_BLOCKS_PER_STEP, nb)
    blk = (1, 1, nsub * A_BLOCK, A_GW)
    cur = lambda off: pl.BlockSpec(blk, lambda b, r, n: (b, r, n, off))
    prev = lambda off: pl.BlockSpec((1, 1, A_BLOCK, A_GW),
                                    lambda b, r, n: (b, r, jnp.maximum(n * nsub - 1, 0), off))
    o_spec = pl.BlockSpec(blk, lambda b, r, n: (b, r, n, 0))
    return pl.pallas_call(
        functools.partial(_a_prompt_kernel, nsub),
        grid=(batch, dil, nb // nsub),
        in_specs=[cur(0), prev(1), cur(1), prev(2), cur(2)],
        out_specs=[o_spec, o_spec],
        out_shape=[jax.ShapeDtypeStruct((batch, dil, m_len, A_GW), F32)] * 2,
        compiler_params=_params(3),
        name=name,
    )(stream, stream, stream, stream, stream)


def _a_sample_kernel(seq, qkv_ref, c0_ref, c1_ref, c2_ref, o_ref):
    rows = A_HEADS * seq
    qkv = qkv_ref[0]
    blockmask = _own_block_mask((rows, A_GW), seq, A_HEAD_DIM)
    outs, lses = [], []
    for g, ((window, dil), c_ref) in enumerate(zip(A_GROUPS, (c0_ref, c1_ref, c2_ref))):
        base = g * 3 * A_GW
        q = qkv[:, base:base + A_GW]
        kn = qkv[:, base + A_GW:base + 2 * A_GW]
        vn = qkv[:, base + 2 * A_GW:base + 3 * A_GW]
        qblk = jnp.where(blockmask, jnp.concatenate([q] * A_HEADS, axis=0), 0.0).astype(BF16)
        buf_len = c_ref.shape[4]
        kct = c_ref[0, 0, 0].astype(BF16)
        vct = c_ref[0, 0, 1].astype(BF16)
        s = _dot(qblk, kct)
        kidx = lax.broadcasted_iota(jnp.int32, (rows, buf_len), 1)
        trow = lax.broadcasted_iota(jnp.int32, (rows, buf_len), 0) & (seq - 1)
        back = buf_len + trow - kidx
        valid = (back <= window) & ((back & (dil - 1)) == 0)
        s = jnp.where(valid, s, NEG_INF)
        zpad = jnp.zeros((A_BLOCK - seq, A_GW), F32)
        knp = jnp.concatenate([kn, zpad], axis=0).astype(BF16)
        vnp = jnp.concatenate([vn, zpad], axis=0).astype(BF16)
        s2 = _dot_nt(qblk, knp)
        u = lax.broadcasted_iota(jnp.int32, (rows, A_BLOCK), 1)
        t2 = lax.broadcasted_iota(jnp.int32, (rows, A_BLOCK), 0) & (seq - 1)
        valid2 = (u <= t2) & (((t2 - u) & (dil - 1)) == 0)
        s2 = jnp.where(valid2, s2, NEG_INF)
        m = jnp.maximum(jnp.max(s, axis=-1, keepdims=True), jnp.max(s2, axis=-1, keepdims=True))
        e = jnp.exp(s - m)
        e2 = jnp.exp(s2 - m)
        l = jnp.sum(e, axis=-1, keepdims=True) + jnp.sum(e2, axis=-1, keepdims=True)
        outs.append((_dot_nt(e.astype(BF16), vct) + _dot(e2.astype(BF16), vnp)) / l)
        lses.append(m + jnp.log(l))
    mx = functools.reduce(jnp.maximum, lses)
    es = [jnp.exp(l - mx) for l in lses]
    den = es[0] + es[1] + es[2]
    merged = (es[0] * outs[0] + es[1] * outs[1] + es[2] * outs[2]) / den
    merged = jnp.where(blockmask, merged, 0.0)
    o = merged[0:seq]
    for hh in range(1, A_HEADS):
        o = o + merged[hh * seq:(hh + 1) * seq]
    o_ref[0] = o


def _a_sample_call(name, qkv, caches, layer, dec_batch, seq):
    cols = qkv.shape[1]
    views = [jnp.transpose(c, (0, 1, 3, 4, 5, 2)).reshape(c.shape[0], c.shape[1], 2, A_GW, c.shape[2])
             for c in caches]
    cspec = lambda v: pl.BlockSpec((1, 1, 2, A_GW, v.shape[4]), lambda b: (layer, b, 0, 0, 0))
    return pl.pallas_call(
        functools.partial(_a_sample_kernel, seq),
        grid=(dec_batch,),
        in_specs=[pl.BlockSpec((1, seq, cols), lambda b: (b, 0, 0))] + [cspec(v) for v in views],
        out_specs=pl.BlockSpec((1, seq, A_GW), lambda b: (b, 0, 0)),
        out_shape=jax.ShapeDtypeStruct((dec_batch, seq, A_GW), F32),
        compiler_params=_params(1),
        name=name,
    )(qkv.reshape(dec_batch, seq, cols), *views)


def _flash_kernel(vheads, finalize, qt_ref, k_ref, vt_ref, *rest):
    extra = rest[:-4]
    o_ref, m_scr, l_scr, acc_scr = rest[-4:]
    i = pl.program_id(1)
    j = pl.program_id(2)
    tq = qt_ref.shape[2]
    tk = k_ref.shape[1]

    @pl.when(j == 0)
    def _():
        m_scr[...] = jnp.full(m_scr.shape, NEG_INF, F32)
        l_scr[...] = jnp.zeros(l_scr.shape, F32)
        acc_scr[...] = jnp.zeros(acc_scr.shape, F32)

    def step(masked):
        if masked:
            causal = (lax.broadcasted_iota(jnp.int32, (tk, tq), 0)
                      <= lax.broadcasted_iota(jnp.int32, (tk, tq), 1))
        for vh, (qs, ks, vs) in enumerate(vheads):
            st = _dot(k_ref[0, :, ks], qt_ref[0, qs, :])
            if masked:
                st = jnp.where(causal, st, NEG_INF)
            m_prev = m_scr[vh]
            m_new = jnp.maximum(m_prev, jnp.max(st, axis=0, keepdims=True))
            alpha = jnp.exp2(m_prev - m_new)
            p = jnp.exp2(st - m_new)
            l_scr[vh] = alpha * l_scr[vh] + jnp.sum(p, axis=0, keepdims=True)
            acc_scr[vh] = alpha * acc_scr[vh] + _dot(vt_ref[0, vs, :], p.astype(BF16))
            m_scr[vh] = m_new

    @pl.when(j < i)
    def _():
        step(False)

    @pl.when(j == i)
    def _():
        step(True)
        finalize(extra, o_ref, l_scr, acc_scr)


def _b_finalize(lam_init, extra, o_ref, l_scr, acc_scr):
    lam_ref, subln_ref = extra
    lam = lam_ref[...]
    for hh in range(B_HEADS):
        o1 = acc_scr[2 * hh] / l_scr[2 * hh]
        o2 = acc_scr[2 * hh + 1] / l_scr[2 * hh + 1]
        od = o1 - lam * o2
        ms = jnp.mean(od * od, axis=0, keepdims=True)
        od = (od * lax.rsqrt(ms + NORM_EPS) * subln_ref[...]) * (1.0 - lam_init)
        o_ref[0, :, hh * B_PAIR:(hh + 1) * B_PAIR] = od.T.astype(o_ref.dtype)


def _c_finalize(extra, o_ref, l_scr, acc_scr):
    for hh in range(C_HEADS):
        o_ref[0, :, hh * C_V:(hh + 1) * C_V] = (acc_scr[hh] / l_scr[hh]).T.astype(o_ref.dtype)


def _flash_call(name, qt, k, vt, extra, vheads, finalize, dv, n_out):
    batch, dq, seq = qt.shape
    tq = min(FLASH_TILE, seq)
    nq = seq // tq
    kv = k.reshape(batch, seq, k.shape[1])
    nvh = len(vheads)
    out = pl.pallas_call(
        functools.partial(_flash_kernel, tuple(vheads), finalize),
        grid=(batch, nq, nq),
        in_specs=[pl.BlockSpec((1, dq, tq), lambda b, i, j: (b, 0, i)),
                  pl.BlockSpec((1, tq, kv.shape[2]), lambda b, i, j: (b, jnp.minimum(i, j), 0)),
                  pl.BlockSpec((1, vt.shape[1], tq), lambda b, i, j: (b, 0, jnp.minimum(i, j)))]
                 + [_const_spec(e.shape) for e in extra],
        out_specs=pl.BlockSpec((1, tq, n_out), lambda b, i, j: (b, i, 0)),
        out_shape=jax.ShapeDtypeStruct((batch, seq, n_out), BF16),
        scratch_shapes=[pltpu.VMEM((nvh, 1, tq), F32), pltpu.VMEM((nvh, 1, tq), F32),
                        pltpu.VMEM((nvh, dv, tq), F32)],
        compiler_params=_params(3),
        name=name,
    )(qt, kv, vt, *extra)
    return out.reshape(batch * seq, n_out)


def _online_update(s, m_scr, l_scr):
    m_prev = m_scr[...]
    m_new = jnp.maximum(m_prev, jnp.max(s, axis=-1, keepdims=True))
    alpha = jnp.exp(m_prev - m_new)
    p = jnp.exp(s - m_new)
    l_scr[...] = alpha * l_scr[...] + jnp.sum(p, axis=-1, keepdims=True)
    m_scr[...] = m_new
    return alpha, p.astype(BF16)


def _new_row_mask(rows, seq):
    u = lax.broadcasted_iota(jnp.int32, (rows, PAGE_SIZE), 1)
    t = lax.broadcasted_iota(jnp.int32, (rows, PAGE_SIZE), 0) & (seq - 1)
    return u <= t


def _b_sample_kernel(n_pages, seq, lam_init, pt_ref, q_ref, kvn_ref, lam_ref, subln_ref, *rest):
    page_refs = rest[:n_pages]
    o_ref, qall_scr, kv_scr, m_scr, l_scr, acc_scr = rest[n_pages:]
    j = pl.program_id(1)
    rows = B_KV_HEADS * 2 * B_REP * seq
    half_rows = rows // B_KV_HEADS
    kw = B_KV_HEADS * B_PAIR
    nsec = 2 * B_KV_HEADS

    @pl.when(j == 0)
    def _():
        q = q_ref[0]
        lane = lax.broadcasted_iota(jnp.int32, (1, B_PAIR), 1)
        zero = jnp.zeros((seq, B_PAIR), F32)
        pieces = []
        for n in range(B_KV_HEADS):
            for c in range(2):
                keep = (lane < B_HEAD_DIM) if c == 0 else (lane >= B_HEAD_DIM)
                for g in range(B_REP):
                    hh = n * B_REP + g
                    qp = jnp.where(keep, q[:, hh * B_PAIR:(hh + 1) * B_PAIR], 0.0)
                    parts = [zero] * B_KV_HEADS
                    parts[n] = qp
                    pieces.append(jnp.concatenate(parts, axis=1))
        qall_scr[...] = jnp.concatenate(pieces, axis=0).astype(BF16)
        m_scr[...] = jnp.full(m_scr.shape, NEG_INF, F32)
        l_scr[...] = jnp.zeros(l_scr.shape, F32)
        acc_scr[...] = jnp.zeros(acc_scr.shape, F32)

    for p in range(n_pages):
        for sec in range(nsec):
            kv_scr[p * PAGE_SIZE:(p + 1) * PAGE_SIZE, sec * B_PAIR:(sec + 1) * B_PAIR] = (
                page_refs[p][0, pl.ds(sec, PAGE_SIZE, stride=nsec), :].astype(BF16))

    first_kv_head = lax.broadcasted_iota(jnp.int32, (rows, 1), 0) < half_rows

    def update(kt, vt, mask):
        s = _dot_nt(qall_scr[...], kt)
        if mask is not None:
            s = jnp.where(mask, s, NEG_INF)
        alpha, p = _online_update(s, m_scr, l_scr)
        r = _dot(p, vt)
        acc_scr[...] = alpha * acc_scr[...] + jnp.where(first_kv_head, r[:, :B_PAIR], r[:, B_PAIR:])

    update(kv_scr[:, :kw], kv_scr[:, kw:], None)

    @pl.when(j == pl.num_programs(1) - 1)
    def _():
        zpad = jnp.zeros((PAGE_SIZE - seq, 2 * kw), F32)
        new = jnp.concatenate([kvn_ref[0], zpad], axis=0).astype(BF16)
        update(new[:, :kw], new[:, kw:], _new_row_mask(rows, seq))
        o = acc_scr[...] / l_scr[...]
        lam = lam_ref[...]
        grp = B_REP * seq
        for n in range(B_KV_HEADS):
            o1 = o[n * 2 * grp:n * 2 * grp + grp]
            o2 = o[n * 2 * grp + grp:(n + 1) * 2 * grp]
            od = _rms_rows(o1 - lam * o2, subln_ref[...]) * (1.0 - lam_init)
            for g in range(B_REP):
                hh = n * B_REP + g
                o_ref[0, :, hh * B_PAIR:(hh + 1) * B_PAIR] = od[g * seq:(g + 1) * seq]


def _page_specs(n_pages, pages_per_seq, block):
    def spec(p):
        return pl.BlockSpec((1,) + block,
                            lambda b, j, pt: (pt[b * pages_per_seq + j * n_pages + p], 0, 0))
    return [spec(p) for p in range(n_pages)]


def _b_sample_call(name, q, kvn, lam, subln, cache, pt_flat, dec_batch, seq, pages_per_seq, lam_init):
    n_pages = min(PAGES_PER_STEP, pages_per_seq)
    width = 2 * B_KV_HEADS * B_PAIR
    rows = B_KV_HEADS * 2 * B_REP * seq
    per_seq = lambda n: pl.BlockSpec((1, seq, n), lambda b, j, pt: (b, 0, 0))
    grid_spec = pltpu.PrefetchScalarGridSpec(
        num_scalar_prefetch=1,
        grid=(dec_batch, pages_per_seq // n_pages),
        in_specs=[per_seq(q.shape[1]), per_seq(width),
                  pl.BlockSpec(lam.shape, lambda b, j, pt: (0, 0)),
                  pl.BlockSpec(subln.shape, lambda b, j, pt: (0, 0))]
                 + _page_specs(n_pages, pages_per_seq, cache.shape[1:]),
        out_specs=per_seq(B_HEADS * B_PAIR),
        scratch_shapes=[pltpu.VMEM((rows, B_KV_HEADS * B_PAIR), BF16),
                        pltpu.VMEM((n_pages * PAGE_SIZE, width), BF16),
                        pltpu.VMEM((rows, 1), F32), pltpu.VMEM((rows, 1), F32),
                        pltpu.VMEM((rows, B_PAIR), F32)])
    out = pl.pallas_call(
        functools.partial(_b_sample_kernel, n_pages, seq, lam_init),
        grid_spec=grid_spec,
        out_shape=jax.ShapeDtypeStruct((dec_batch, seq, B_HEADS * B_PAIR), F32),
        compiler_params=_params(2),
        name=name,
    )(pt_flat, q.reshape(dec_batch, seq, -1), kvn.reshape(dec_batch, seq, -1), lam, subln,
      *([cache] * n_pages))
    return out.reshape(dec_batch * seq, -1)


def _c_sample_kernel(n_pages, seq, pt_ref, q_ref, latn_ref, gk_ref, wukt_ref, wuk_ref, wuv_ref, *rest):
    page_refs = rest[:n_pages]
    o_ref, lhs_scr, qr_scr, lat_scr, m_scr, l_scr, acc_scr = rest[n_pages:]
    j = pl.program_id(1)
    rows = C_HEADS * seq
    nkx = C_HEADS * C_NOPE

    @pl.when(j == 0)
    def _():
        q = q_ref[0]
        zero = jnp.zeros((seq, C_NOPE), F32)
        blocks, ropes = [], []
        for hh in range(C_HEADS):
            parts = [zero] * C_HEADS
            parts[hh] = q[:, hh * C_QK_PAD:hh * C_QK_PAD + C_NOPE] * gk_ref[...]
            blocks.append(jnp.concatenate(parts, axis=1))
            ropes.append(q[:, hh * C_QK_PAD + C_NOPE:hh * C_QK_PAD + C_NOPE + C_ROPE])
        qn = jnp.concatenate(blocks, axis=0).astype(BF16)
        lhs_scr[:nkx] = wukt_ref[...]
        lhs_scr[nkx:] = _dot_nt(qn, wuk_ref[...]).astype(BF16)
        qr_scr[...] = jnp.concatenate(ropes, axis=0).astype(BF16)
        m_scr[...] = jnp.full(m_scr.shape, NEG_INF, F32)
        l_scr[...] = jnp.zeros(l_scr.shape, F32)
        acc_scr[...] = jnp.zeros(acc_scr.shape, F32)

    for p in range(n_pages):
        lat_scr[:, p * PAGE_SIZE:(p + 1) * PAGE_SIZE] = page_refs[p][0].astype(BF16)

    def scores(lt):
        ckvt = lt[:C_KV_LORA]
        kpet = lt[C_KV_LORA:]
        nk = lt.shape[1]
        scale_rows = []
        for hh in range(C_HEADS):
            kx = _dot(lhs_scr[hh * C_NOPE:(hh + 1) * C_NOPE, :], ckvt)
            ms = jnp.sum(kx * kx, axis=0, keepdims=True) * (1.0 / C_NOPE)
            scale_rows.append(jnp.broadcast_to(lax.rsqrt(ms + NORM_EPS), (seq, nk)))
        sq = _dot(lhs_scr[nkx:, :], ckvt)
        return sq * jnp.concatenate(scale_rows, axis=0) + _dot(qr_scr[...], kpet)

    def attend(s, ckvt):
        alpha, p = _online_update(s, m_scr, l_scr)
        acc_scr[...] = alpha * acc_scr[...] + _dot_nt(p, ckvt)

    sub = min(C_KEY_SUBTILE, n_pages * PAGE_SIZE)
    s_all = jnp.concatenate([scores(lat_scr[:, t * sub:(t + 1) * sub])
                             for t in range(n_pages * PAGE_SIZE // sub)], axis=1)
    attend(s_all, lat_scr[:C_KV_LORA, :])

    @pl.when(j == pl.num_programs(1) - 1)
    def _():
        new = latn_ref[0].astype(BF16)
        attend(jnp.where(_new_row_mask(rows, seq), scores(new), NEG_INF), new[:C_KV_LORA])
        o_lat = (acc_scr[...] / l_scr[...]).astype(BF16)
        r = _dot(o_lat, wuv_ref[...])
        r = jnp.where(_own_block_mask((rows, C_HEADS * C_V), seq, C_V), r, 0.0)
        o = r[0:seq]
        for hh in range(1, C_HEADS):
            o = o + r[hh * seq:(hh + 1) * seq]
        o_ref[0] = o


def _c_sample_call(name, q, latn_t, gk, wukt, wuk, wuv, cache, pt_flat, dec_batch, seq, pages_per_seq):
    n_pages = min(PAGES_PER_STEP, pages_per_seq)
    rows = C_HEADS * seq
    per_seq = lambda n: pl.BlockSpec((1, seq, n), lambda b, j, pt: (b, 0, 0))
    const = lambda a: pl.BlockSpec(a.shape, lambda b, j, pt: (0, 0))
    grid_spec = pltpu.PrefetchScalarGridSpec(
        num_scalar_prefetch=1,
        grid=(dec_batch, pages_per_seq // n_pages),
        in_specs=[per_seq(q.shape[1]),
                  pl.BlockSpec((1, C_LAT, PAGE_SIZE), lambda b, j, pt: (b, 0, 0)),
                  const(gk), const(wukt), const(wuk), const(wuv)]
                 + _page_specs(n_pages, pages_per_seq, cache.shape[1:]),
        out_specs=per_seq(C_HEADS * C_V),
        scratch_shapes=[pltpu.VMEM((C_HEADS * C_NOPE + rows, C_KV_LORA), BF16), pltpu.VMEM((rows, C_ROPE), BF16),
                        pltpu.VMEM((C_LAT, n_pages * PAGE_SIZE), BF16),
                        pltpu.VMEM((rows, 1), F32), pltpu.VMEM((rows, 1), F32),
                        pltpu.VMEM((rows, C_KV_LORA), F32)])
    out = pl.pallas_call(
        functools.partial(_c_sample_kernel, n_pages, seq),
        grid_spec=grid_spec,
        out_shape=jax.ShapeDtypeStruct((dec_batch, seq, C_HEADS * C_V), F32),
        compiler_params=_params(2),
        name=name,
    )(pt_flat, q.reshape(dec_batch, seq, -1), latn_t, gk, wukt, wuk, wuv, *([cache] * n_pages))
    return out.reshape(dec_batch * seq, -1)


def _block_mean_matrix(widths):
    m = np.zeros((LANE_CHUNK, LANE_CHUNK), np.float32)
    o = 0
    for w in widths:
        m[o:o + w, o:o + w] = 1.0 / w
        o += w
    return jnp.asarray(m, BF16)


def _rope_tables(pos):
    inv = ROPE_THETA ** (-jnp.arange(ROPE_HALF, dtype=F32) / ROPE_HALF)
    ang = pos.astype(F32)[:, None] * inv[None, :]
    cos, sin = jnp.cos(ang), jnp.sin(ang)
    return jnp.concatenate([cos, cos], axis=1), jnp.concatenate([-sin, sin], axis=1)


def _tables(pos):
    cos64, sin64 = _rope_tables(pos)
    n = pos.shape[0]
    one, zero = jnp.ones((n, ROPE_DIM), F32), jnp.zeros((n, ROPE_DIM), F32)
    return dict(
        full=(jnp.tile(cos64, (1, 4)), jnp.tile(sin64, (1, 4))),
        cq=(jnp.concatenate([one, one, cos64, one], 1), jnp.concatenate([zero, zero, sin64, zero], 1)),
        ck=(jnp.concatenate([cos64, one], 1), jnp.concatenate([sin64, zero], 1)))


def kernel(x_prompt, x_sample, cache_a_g0, cache_a_g1, cache_a_g2, cache_b_kv, cache_c_latent,
           state_ffn_conv, page_table, norm_attn, norm_ffn, a_w_qkv, a_q_norm, a_k_norm, a_w_o,
           b_w_qkv, b_q_norm, b_k_norm, b_lambda, b_subln, b_w_o, c_w_down, c_q_lora_norm, c_w_uq,
           c_kv_norm, c_w_ukv, c_q_norm, c_k_norm, c_w_o, ffn_w_gate, ffn_w_up, ffn_conv_w, ffn_conv_b,
           ffn_w_down):
    batch, seq, d = x_prompt.shape
    dec_batch, dec_seq, _ = x_sample.shape
    depth = norm_attn.shape[0]
    pages_per_seq = page_table.shape[1]
    past = pages_per_seq * PAGE_SIZE
    n_pool = cache_b_kv.shape[1] if cache_b_kv.shape[0] else cache_c_latent.shape[1]
    a_caches = (cache_a_g0, cache_a_g1, cache_a_g2)
    for (window, dil), c in zip(A_GROUPS, a_caches):
        assert c.shape[2] == window and seq % (dil * A_BLOCK) == 0
    assert dec_seq == 8 and A_BLOCK == PAGE_SIZE

    tp, ts = batch * seq, dec_batch * dec_seq
    tm_p, tm_s = min(TOKEN_TILE, seq), min(TOKEN_TILE, ts)
    xp = x_prompt.reshape(tp, d)
    xs = x_sample.reshape(ts, d)
    tab_p = _tables(jnp.arange(seq))
    tab_s = _tables(jnp.tile(past + jnp.arange(dec_seq), tm_s // dec_seq))
    pt_flat = page_table.reshape(-1)
    bmat64 = _block_mean_matrix((64, 64, 64, 64))
    bmat_cq = _block_mean_matrix((128, 64, 64))
    row = lambda v: v.reshape(1, -1).astype(F32)
    ng = len(A_GROUPS)
    dils = [dil for _, dil in A_GROUPS]

    a_p = [[] for _ in A_GROUPS]
    a_s = [[] for _ in A_GROUPS]
    b_p, b_s, c_p, c_s, f_p, f_s = [], [], [], [], [], []
    ia = ib = ic = 0
    for i in range(depth):
        g_attn = row(norm_attn[i])
        if i % N_MIXERS == 0:
            w = a_w_qkv[ia].astype(BF16)
            ncol = w.shape[1]
            gains = []
            for g in range(ng):
                gains += [jnp.tile(a_q_norm[ia, g], A_HEADS) * A_SCALE, jnp.tile(a_k_norm[ia, g], A_HEADS),
                          jnp.ones((A_GW,), F32)]
            gain = row(jnp.concatenate(gains))
            nch = ncol // LANE_CHUNK
            per_group = 3 * A_GW // LANE_CHUNK
            plan_p, plan_s = [], []
            for c in range(nch):
                g, cc = divmod(c, per_group)
                dests = [("stream", g, cc * LANE_CHUNK, dils[g])]
                if cc >= 2:
                    dests.append(("keep_t", ng + g, (cc - 2) // 2, cc % 2, min(A_GROUPS[g][0], seq)))
                plan_p.append((cc < 4, tuple(dests)))
                plan_s.append((cc < 4, (("rows", 0, c * LANE_CHUNK),)))
            outs_p = ([("stream", F32, 3 * A_GW, dil) for dil in dils]
                      + [("keep_t", F32, min(window, seq)) for window, _ in A_GROUPS])
            res = _proj_call(f"a{ia}_proj_p", xp, g_attn, w, bmat64, gain, *tab_p['full'], plan_p, outs_p,
                             batch, seq, tm_p)
            streams, keeps = res[:ng], res[ng:]
            qkv_s = _proj_call(f"a{ia}_proj_s", xs, g_attn, w, bmat64, gain, *tab_s['full'], plan_s,
                               [("rows", F32, ncol)], 1, ts, tm_s)[0]
            outs, lses = [], []
            for g in range(ng):
                o, l = _a_prompt_call(f"a{ia}_attn_p{g}", streams[g])
                outs.append(o)
                lses.append(l)
                keep = keeps[g].shape[3]
                a_p[g].append(jnp.transpose(keeps[g].reshape(batch, 2, A_HEADS, A_HEAD_DIM, keep), (0, 4, 1, 2, 3)))
                a_s[g].append(qkv_s.reshape(dec_batch, dec_seq, ng, 3, A_HEADS, A_HEAD_DIM)[:, :, g, 1:3])
            wo = a_w_o[ia].astype(BF16)
            xp = _a_out_call(f"a{ia}_out_p", outs, lses, dils, xp, wo, seq, tm_p)
            os_ = _a_sample_call(f"a{ia}_attn_s", qkv_s, a_caches, ia, dec_batch, dec_seq)
            xs = _out_call(f"a{ia}_out_s", os_.reshape(ts, A_GW), xs, wo, tm_s)
            ia += 1
        elif i % N_MIXERS == 1:
            lam_init = 0.8 - 0.6 * math.exp(-0.3 * i)
            lpf = b_lambda[ib].astype(F32)
            lam = (jnp.exp(jnp.sum(lpf[0] * lpf[1])) - jnp.exp(jnp.sum(lpf[2] * lpf[3])) + lam_init).reshape(1, 1)
            w = b_w_qkv[ib].astype(BF16)
            nq = B_HEADS * B_PAIR
            nk = B_KV_HEADS * B_PAIR
            gains = lambda q_scale: row(jnp.concatenate([
                jnp.tile(b_q_norm[ib], 2 * B_HEADS) * q_scale, jnp.tile(b_k_norm[ib], 2 * B_KV_HEADS),
                jnp.ones((nk,), F32)]))
            gain, gain_p = gains(B_SCALE), gains(B_SCALE * LOG2_E)
            nqc = nq // LANE_CHUNK
            assert nk == LANE_CHUNK
            plan_p = ([(True, (("cols_comp", 0, c * LANE_CHUNK),)) for c in range(nqc)]
                      + [(True, (("rows", 1, 0), ("rows", 2, 0))), (False, (("rows", 1, nk), ("cols", 3, 0)))])
            outs_p = [("cols", BF16, 2 * nq), ("rows", F32, 2 * nk), ("rows", BF16, nk), ("cols", BF16, nk)]
            qt_p, kv_p, kb_p, vt_p = _proj_call(f"b{ib}_proj_p", xp, g_attn, w, bmat64, gain_p, *tab_p['full'],
                                                plan_p, outs_p, batch, seq, tm_p)
            plan_s = ([(True, (("rows", 0, c * LANE_CHUNK),)) for c in range(nqc)]
                      + [(True, (("rows", 1, 0),)), (False, (("rows", 1, nk),))])
            q_s, kv_s = _proj_call(f"b{ib}_proj_s", xs, g_attn, w, bmat64, gain, *tab_s['full'], plan_s,
                                   [("rows", F32, nq), ("rows", F32, 2 * nk)], 1, ts, tm_s)
            b_p.append(kv_p.reshape(batch, seq, 2, B_KV_HEADS, B_PAIR))
            b_s.append(kv_s.reshape(dec_batch, dec_seq, 2, B_KV_HEADS, B_PAIR))
            vheads = [(slice(vh * B_PAIR, (vh + 1) * B_PAIR),
                       slice((vh // (2 * B_REP)) * B_PAIR, (vh // (2 * B_REP) + 1) * B_PAIR),
                       slice((vh // (2 * B_REP)) * B_PAIR, (vh // (2 * B_REP) + 1) * B_PAIR))
                      for vh in range(2 * B_HEADS)]
            o_p = _flash_call(f"b{ib}_attn_p", qt_p, kb_p, vt_p, [lam, b_subln[ib].reshape(-1, 1).astype(F32)],
                              vheads, functools.partial(_b_finalize, lam_init), B_PAIR, nq)
            cache = cache_b_kv.reshape(cache_b_kv.shape[0] * n_pool, PAGE_SIZE * 2 * B_KV_HEADS, B_PAIR)
            o_s = _b_sample_call(f"b{ib}_attn_s", q_s, kv_s, lam, row(b_subln[ib]), cache, pt_flat + ib * n_pool,
                                 dec_batch, dec_seq, pages_per_seq, lam_init)
            wo = b_w_o[ib].astype(BF16)
            xp = _out_call(f"b{ib}_out_p", o_p, xp, wo, tm_p)
            xs = _out_call(f"b{ib}_out_s", o_s, xs, wo, tm_s)
            ib += 1
        else:
            wd = jnp.pad(c_w_down[ic], ((0, 0), (0, LANE - C_ROPE))).astype(BF16)
            wuq = c_w_uq[ic].reshape(C_Q_LORA, C_HEADS, C_NOPE + C_ROPE)
            wuq = jnp.pad(wuq, ((0, 0), (0, 0), (0, C_QK_PAD - C_NOPE - C_ROPE)))
            wuq = wuq.reshape(C_Q_LORA, C_HEADS * C_QK_PAD).astype(BF16)
            wukv = c_w_ukv[ic].reshape(C_KV_LORA, C_HEADS, C_NOPE + C_V)
            wuk = wukv[..., :C_NOPE].reshape(C_KV_LORA, C_HEADS * C_NOPE).astype(BF16)
            wuv = wukv[..., C_NOPE:].reshape(C_KV_LORA, C_HEADS * C_V).astype(BF16)
            qn, kn = c_q_norm[ic], c_k_norm[ic]
            gq = row(jnp.concatenate([qn, jnp.zeros((C_QK_PAD - C_NOPE - C_ROPE,), F32)]) * C_SCALE)
            gq_p = gq * LOG2_E
            kpg = row(jnp.concatenate([kn[C_NOPE:], jnp.zeros((LANE - C_ROPE,), F32)]))
            gk = row(kn[:C_NOPE])
            cproj = lambda nm, prompt, x, tab, nb, sq, tm: _c_proj_call(
                nm, prompt, x, g_attn, wd, row(c_q_lora_norm[ic]), row(c_kv_norm[ic]), kpg, tab['ck'][0],
                tab['ck'][1], wuq, bmat_cq, gq_p if prompt else gq, tab['cq'][0], tab['cq'][1], wuk, gk, wuv,
                nb, sq, tm)
            lat_p, qt_p, k_p, vt_p = cproj(f"c{ic}_proj_p", True, xp, tab_p, batch, seq, tm_p)
            lat_s, q_s = cproj(f"c{ic}_proj_s", False, xs, tab_s, 1, ts, tm_s)
            c_p.append(lat_p.reshape(batch, seq, C_LAT))
            c_s.append(lat_s.reshape(dec_batch, dec_seq, C_LAT))
            vheads = [(slice(hh * C_QK_PAD, (hh + 1) * C_QK_PAD), slice(hh * C_QK_PAD, (hh + 1) * C_QK_PAD),
                       slice(hh * C_V, (hh + 1) * C_V)) for hh in range(C_HEADS)]
            o_p = _flash_call(f"c{ic}_attn_p", qt_p, k_p, vt_p, [], vheads, _c_finalize, C_V, C_HEADS * C_V)
            cache = jnp.swapaxes(cache_c_latent, 2, 3).reshape(cache_c_latent.shape[0] * n_pool, C_LAT, PAGE_SIZE)
            latn_t = jnp.pad(jnp.swapaxes(lat_s.reshape(dec_batch, dec_seq, C_LAT), 1, 2),
                             ((0, 0), (0, 0), (0, PAGE_SIZE - dec_seq)))
            o_s = _c_sample_call(f"c{ic}_attn_s", q_s, latn_t, gk, wuk.T, wuk, wuv, cache,
                                 pt_flat + ic * n_pool, dec_batch, dec_seq, pages_per_seq)
            wo = c_w_o[ic].astype(BF16)
            xp = _out_call(f"c{ic}_out_p", o_p, xp, wo, tm_p)
            xs = _out_call(f"c{ic}_out_s", o_s, xs, wo, tm_s)
            ic += 1

        g_ffn = row(norm_ffn[i])
        wg, wu, wdn = ffn_w_gate[i].astype(BF16), ffn_w_up[i].astype(BF16), ffn_w_down[i].astype(BF16)
        cw, cb = ffn_conv_w[i].astype(F32), row(ffn_conv_b[i])
        xp, st_p = _ffn_prompt_call(f"ffn{i}_p", xp, seq, g_ffn, wg, wu, cw, cb, wdn, tm_p)
        ctx = state_ffn_conv[i]
        e1 = jnp.pad(ctx[:, 1:2], ((0, 0), (0, dec_seq - 1), (0, 0))).reshape(ts, D_FF)
        e2 = jnp.pad(ctx, ((0, 0), (0, dec_seq - 2), (0, 0))).reshape(ts, D_FF)
        xs, gate_s = _ffn_sample_call(f"ffn{i}_s", xs, dec_seq, e1, e2, g_ffn, wg, wu, cw, cb, wdn,
                                      min(FFN_SAMPLE_TILE, ts))
        f_p.append(st_p[:, 8 - (CONV_W - 1):])
        f_s.append(gate_s.reshape(dec_batch, dec_seq, D_FF)[:, dec_seq - (CONV_W - 1):])

    return (xp.reshape(batch, seq, d), xs.reshape(dec_batch, dec_seq, d),
            jnp.stack(a_p[0]), jnp.stack(a_p[1]), jnp.stack(a_p[2]),
            jnp.stack(b_p), jnp.stack(c_p), jnp.stack(f_p),
            jnp.stack(a_s[0]), jnp.stack(a_s[1]), jnp.stack(a_s[2]),
            jnp.stack(b_s), jnp.stack(c_s), jnp.stack(f_s))
```

```python
import functools
import math

import numpy as np
import jax
import jax.numpy as jnp
from jax import lax
from jax.experimental import pallas as pl
from jax.experimental.pallas import tpu as pltpu

F32 = jnp.float32
BF16 = jnp.bfloat16

D_MODEL = 1024
PAGE_SIZE = 128
N_MIXERS = 3
ROPE_THETA = 10000.0
NORM_EPS = 1e-6

A_GROUPS = ((128, 1), (512, 4), (2048, 16))
A_HEADS = 8
A_HEAD_DIM = 64
A_SCALE = A_HEAD_DIM ** -0.5
A_BLOCK = 128
A_GW = A_HEADS * A_HEAD_DIM
A_BLOCKS_PER_STEP = 4

B_HEADS = 8
B_KV_HEADS = 2
B_HEAD_DIM = 64
B_SCALE = B_HEAD_DIM ** -0.5
B_REP = B_HEADS // B_KV_HEADS
B_PAIR = 2 * B_HEAD_DIM

C_HEADS = 8
C_Q_LORA = 384
C_KV_LORA = 256
C_NOPE = 128
C_ROPE = 64
C_V = 128
C_SCALE = (C_NOPE + C_ROPE) ** -0.5
C_QK_PAD = 256
C_LAT = C_KV_LORA + C_ROPE

D_FF = 2816
CONV_W = 3
FFN_CHUNK = 256

ROPE_DIM = 64
ROPE_HALF = 32
LANE = 128
LANE_CHUNK = 256
TOKEN_TILE = 512
PROJ_DOT_BATCH = 3
FFN_SAMPLE_TILE = 256
B_FLASH_TILE = 512
C_FLASH_TILE = 1024
LOG2_E = math.log2(math.e)
PAGES_PER_STEP = 32
C_KEY_SUBTILE = 512
NEG_INF = float("-inf")
VMEM_LIMIT = 52 * 1024 * 1024


def _dot(a, b):
    return jnp.dot(a, b, preferred_element_type=F32)


def _dot_nt(a, b):
    return lax.dot_general(a, b, (((1,), (1,)), ((), ())), preferred_element_type=F32)


def _rms_rows(x, g):
    ms = jnp.mean(x * x, axis=-1, keepdims=True)
    return x * lax.rsqrt(ms + NORM_EPS) * g


def _seg_mean_sq(y, bmat):
    y2 = y * y
    hi = y2.astype(BF16)
    lo = (y2 - hi.astype(F32)).astype(BF16)
    return _dot(hi, bmat) + _dot(lo, bmat)


def _lo_half_mask(shape):
    return (lax.broadcasted_iota(jnp.int32, shape, len(shape) - 1) & (ROPE_DIM - 1)) < ROPE_HALF


def _swap_halves(y, lo_mask):
    n = y.shape[-1]
    return jnp.where(lo_mask, pltpu.roll(y, n - ROPE_HALF, 1), pltpu.roll(y, ROPE_HALF, 1))


def _norm_rope(y, bmat, gain, cos, sin, lo_mask):
    yn = y * lax.rsqrt(_seg_mean_sq(y, bmat) + NORM_EPS) * gain
    return yn * cos + _swap_halves(yn, lo_mask) * sin


def _const_spec(shape):
    nd = len(shape)
    return pl.BlockSpec(shape, lambda *_: (0,) * nd, pipeline_mode=pl.Buffered(1))


def _params(n_axes):
    return pltpu.CompilerParams(dimension_semantics=("arbitrary",) * n_axes,
                                vmem_limit_bytes=VMEM_LIMIT)


def _own_block_mask(shape, rows_per_head, cols_per_head):
    row_head = lax.broadcasted_iota(jnp.int32, shape, 0) >> int(math.log2(rows_per_head))
    col_head = lax.broadcasted_iota(jnp.int32, shape, 1) >> int(math.log2(cols_per_head))
    return row_head == col_head


def _proj_kernel(plan, tiles_per_seq, n_out, x_ref, g_ref, w_ref, bmat_ref, gain_ref, cos_ref, sin_ref, *rest):
    o_refs = rest[:n_out]
    ybuf = rest[n_out] if len(rest) > n_out else None
    tm = x_ref.shape[0]
    h = _rms_rows(x_ref[...], g_ref[...]).astype(BF16)
    cos = cos_ref[...]
    sin = sin_ref[...]
    lo_mask = _lo_half_mask(cos.shape)
    tile_in_seq = pl.program_id(0) % tiles_per_seq
    ys = {}
    for c, (mode, dests) in enumerate(plan):
        if c % PROJ_DOT_BATCH == 0:
            ys = {cc: _dot(h, w_ref[:, cc * LANE_CHUNK:(cc + 1) * LANE_CHUNK])
                  for cc in range(c, min(c + PROJ_DOT_BATCH, len(plan)))}
        sl = slice(c * LANE_CHUNK, (c + 1) * LANE_CHUNK)
        y = ys[c]
        if mode:
            y = _norm_rope(y, bmat_ref[...], gain_ref[:, sl], cos, sin, lo_mask)
        for dest in dests:
            kind, oi = dest[0], dest[1]
            o_ref = o_refs[oi]
            if kind == "rows":
                o_ref[:, dest[2]:dest[2] + LANE_CHUNK] = y.astype(o_ref.dtype)
            elif kind == "cols":
                o_ref[0, dest[2]:dest[2] + LANE_CHUNK, :] = y.T.astype(o_ref.dtype)
            elif kind == "cols_comp":
                yt = y.T
                first = lax.broadcasted_iota(jnp.int32, (B_PAIR, 1), 0) < B_HEAD_DIM
                for pair in range(LANE_CHUNK // B_PAIR):
                    blk = yt[pair * B_PAIR:(pair + 1) * B_PAIR]
                    base = 2 * dest[2] + pair * 2 * B_PAIR
                    o_ref[0, base:base + B_PAIR, :] = jnp.where(first, blk, 0.0).astype(o_ref.dtype)
                    o_ref[0, base + B_PAIR:base + 2 * B_PAIR, :] = jnp.where(first, 0.0, blk).astype(o_ref.dtype)
            elif kind == "stream":
                col, dil = dest[2], dest[3]
                if dil == 1:
                    o_ref[0, 0, :, col:col + LANE_CHUNK] = y
                else:
                    for part in range(LANE_CHUNK // LANE):
                        ybuf[part] = y[:, part * LANE:(part + 1) * LANE]
                    for r in range(dil):
                        for part in range(LANE_CHUNK // LANE):
                            o_ref[0, r, :, col + part * LANE:col + (part + 1) * LANE] = (
                                ybuf[part, pl.ds(r, tm // dil, stride=dil), :])
            else:
                kv, half, keep = dest[2], dest[3], dest[4]
                width = min(keep, tm)

                @pl.when(tile_in_seq >= tiles_per_seq - max(keep // tm, 1))
                def _(y=y, o_ref=o_ref, kv=kv, half=half, width=width):
                    o_ref[0, kv, half * LANE_CHUNK:(half + 1) * LANE_CHUNK, :] = y.T[:, tm - width:]


def _proj_call(name, x, g, w, bmat, gain, cos, sin, plan, outs, batch, seq, tm):
    t, d = x.shape
    n = w.shape[1]
    nrep = cos.shape[0] // tm
    tps = seq // tm
    out_specs, out_shapes = [], []
    need_buf = False
    for o in outs:
        kind, dt = o[0], o[1]
        if kind == "rows":
            out_specs.append(pl.BlockSpec((tm, o[2]), lambda i: (i, 0)))
            out_shapes.append(jax.ShapeDtypeStruct((t, o[2]), dt))
        elif kind == "cols":
            out_specs.append(pl.BlockSpec((1, o[2], tm), lambda i: (i // tps, 0, i % tps)))
            out_shapes.append(jax.ShapeDtypeStruct((batch, o[2], seq), dt))
        elif kind == "stream":
            dil = o[3]
            need_buf = need_buf or dil > 1
            out_specs.append(pl.BlockSpec((1, dil, tm // dil, o[2]), lambda i: (i // tps, 0, i % tps, 0)))
            out_shapes.append(jax.ShapeDtypeStruct((batch, dil, seq // dil, o[2]), dt))
        else:
            keep = o[2]
            first = tps - max(keep // tm, 1)
            out_specs.append(pl.BlockSpec((1, 2, A_GW, min(keep, tm)),
                                          lambda i, first=first: (i // tps, 0, 0, jnp.maximum(i % tps - first, 0))))
            out_shapes.append(jax.ShapeDtypeStruct((batch, 2, A_GW, keep), dt))
    scratch = [pltpu.VMEM((LANE_CHUNK // LANE, tm, LANE), F32)] if need_buf else []
    return pl.pallas_call(
        functools.partial(_proj_kernel, tuple(plan), tps, len(outs)),
        grid=(t // tm,),
        in_specs=[pl.BlockSpec((tm, d), lambda i: (i, 0)),
                  _const_spec((1, d)), _const_spec((d, n)), _const_spec(bmat.shape), _const_spec((1, n)),
                  pl.BlockSpec((tm, LANE_CHUNK), lambda i: (i % nrep, 0)),
                  pl.BlockSpec((tm, LANE_CHUNK), lambda i: (i % nrep, 0))],
        out_specs=out_specs,
        out_shape=out_shapes,
        scratch_shapes=scratch,
        compiler_params=_params(1),
        name=name,
    )(x, g, w, bmat, gain, cos, sin)


def _c_proj_kernel(prompt, x_ref, g_ref, wd_ref, qln_ref, kvn_ref, kpg_ref, cosk_ref, sink_ref,
                   wuq_ref, bq_ref, gq_ref, cosq_ref, sinq_ref, wuk_ref, gk_ref, wuv_ref,
                   lat_ref, q_ref, *kv_refs):
    h = _rms_rows(x_ref[...], g_ref[...]).astype(BF16)
    d = _dot(h, wd_ref[...])
    cq = _rms_rows(d[:, :C_Q_LORA], qln_ref[...])
    ckv = _rms_rows(d[:, C_Q_LORA:C_Q_LORA + C_KV_LORA], kvn_ref[...])
    kp = d[:, C_Q_LORA + C_KV_LORA:]
    ms = jnp.sum(kp * kp, axis=-1, keepdims=True) * (1.0 / C_ROPE)
    kpn = kp * lax.rsqrt(ms + NORM_EPS) * kpg_ref[...]
    kpe = kpn * cosk_ref[...] + _swap_halves(kpn, _lo_half_mask(kpn.shape)) * sink_ref[...]
    lat_ref[:, :C_KV_LORA] = ckv
    lat_ref[:, C_KV_LORA:] = kpe[:, :C_ROPE]
    cqb = cq.astype(BF16)
    cosq = cosq_ref[...]
    sinq = sinq_ref[...]
    lo_mask = _lo_half_mask(cosq.shape)
    for hh in range(C_HEADS):
        sq = slice(hh * C_QK_PAD, (hh + 1) * C_QK_PAD)
        q = _norm_rope(_dot(cqb, wuq_ref[:, sq]), bq_ref[...], gq_ref[...], cosq, sinq, lo_mask)
        if prompt:
            q_ref[0, sq, :] = q.T.astype(q_ref.dtype)
        else:
            q_ref[:, sq] = q
    if prompt:
        k_ref, vt_ref = kv_refs
        ckvb = ckv.astype(BF16)
        kpeb = kpe.astype(BF16)
        for hh in range(C_HEADS):
            yk = _dot(ckvb, wuk_ref[:, hh * C_NOPE:(hh + 1) * C_NOPE])
            k_ref[:, hh * C_QK_PAD:hh * C_QK_PAD + C_NOPE] = _rms_rows(yk, gk_ref[...]).astype(BF16)
            k_ref[:, hh * C_QK_PAD + C_NOPE:(hh + 1) * C_QK_PAD] = kpeb
        for c in range(C_HEADS * C_V // LANE_CHUNK):
            sl = slice(c * LANE_CHUNK, (c + 1) * LANE_CHUNK)
            vt_ref[0, sl, :] = _dot(ckvb, wuv_ref[:, sl]).T.astype(BF16)


def _c_proj_call(name, prompt, x, g, wd, qln, kvn, kpg, cosk, sink, wuq, bq, gq, cosq, sinq, wuk, gk, wuv,
                 batch, seq, tm):
    t, d = x.shape
    nrep = cosk.shape[0] // tm
    tps = seq // tm
    tab = lambda w: pl.BlockSpec((tm, w), lambda i: (i % nrep, 0))
    row = lambda w: pl.BlockSpec((tm, w), lambda i: (i, 0))
    col = lambda n: pl.BlockSpec((1, n, tm), lambda i: (i // tps, 0, i % tps))
    nq = C_HEADS * C_QK_PAD
    nv = C_HEADS * C_V
    if prompt:
        out_specs = [row(C_LAT), col(nq), row(nq), col(nv)]
        out_shape = [jax.ShapeDtypeStruct((t, C_LAT), F32), jax.ShapeDtypeStruct((batch, nq, seq), BF16),
                     jax.ShapeDtypeStruct((t, nq), BF16), jax.ShapeDtypeStruct((batch, nv, seq), BF16)]
    else:
        out_specs = [row(C_LAT), row(nq)]
        out_shape = [jax.ShapeDtypeStruct((t, C_LAT), F32), jax.ShapeDtypeStruct((t, nq), F32)]
    return pl.pallas_call(
        functools.partial(_c_proj_kernel, prompt),
        grid=(t // tm,),
        in_specs=[row(d), _const_spec((1, d)), _const_spec(wd.shape), _const_spec(qln.shape),
                  _const_spec(kvn.shape), _const_spec(kpg.shape), tab(LANE), tab(LANE),
                  _const_spec(wuq.shape), _const_spec(bq.shape), _const_spec(gq.shape),
                  tab(LANE_CHUNK), tab(LANE_CHUNK),
                  _const_spec(wuk.shape), _const_spec(gk.shape), _const_spec(wuv.shape)],
        out_specs=out_specs,
        out_shape=out_shape,
        compiler_params=_params(1),
        name=name,
    )(x, g, wd, qln, kvn, kpg, cosk, sink, wuq, bq, gq, cosq, sinq, wuk, gk, wuv)


def _out_kernel(a_ref, x_ref, w_ref, y_ref):
    y_ref[...] = x_ref[...] + _dot(a_ref[...].astype(BF16), w_ref[...])


def _out_call(name, a, x, w, tm):
    t, d = x.shape
    row = lambda n: pl.BlockSpec((tm, n), lambda i: (i, 0))
    return pl.pallas_call(
        _out_kernel,
        grid=(t // tm,),
        in_specs=[row(w.shape[0]), row(d), _const_spec(w.shape)],
        out_specs=row(d),
        out_shape=jax.ShapeDtypeStruct((t, d), F32),
        compiler_params=_params(1),
        name=name,
    )(a, x, w)


def _a_out_kernel(dils, *refs):
    ng = len(dils)
    o_refs, l_refs = refs[:ng], refs[ng:2 * ng]
    x_ref, w_ref, y_ref, obuf, lbuf = refs[2 * ng:]
    tm = x_ref.shape[0]
    nparts = A_GW // LANE
    outs, lses = [], []
    for g, dil in enumerate(dils):
        if dil == 1:
            outs.append(o_refs[g][0, 0])
            lses.append(l_refs[g][0, 0])
            continue
        for src, buf in ((o_refs[g], obuf), (l_refs[g], lbuf)):
            for r in range(dil):
                for part in range(nparts):
                    buf[g, part, pl.ds(r, tm // dil, stride=dil), :] = src[0, r, :, part * LANE:(part + 1) * LANE]
        outs.append(jnp.concatenate([obuf[g, part] for part in range(nparts)], axis=1))
        lses.append(jnp.concatenate([lbuf[g, part] for part in range(nparts)], axis=1))
    mx = functools.reduce(jnp.maximum, lses)
    es = [jnp.exp(l - mx) for l in lses]
    den = functools.reduce(lambda a, b: a + b, es)
    num = functools.reduce(lambda a, b: a + b, [e * o for e, o in zip(es, outs)])
    y_ref[...] = x_ref[...] + _dot((num / den).astype(BF16), w_ref[...])


def _a_out_call(name, outs, lses, dils, x, w, seq, tm):
    t, d = x.shape
    tps = seq // tm
    stream = lambda dil: pl.BlockSpec((1, dil, tm // dil, A_GW), lambda i: (i // tps, 0, i % tps, 0))
    row = lambda n: pl.BlockSpec((tm, n), lambda i: (i, 0))
    ng = len(dils)
    buf = pltpu.VMEM((ng, A_GW // LANE, tm, LANE), F32)
    return pl.pallas_call(
        functools.partial(_a_out_kernel, tuple(dils)),
        grid=(t // tm,),
        in_specs=[stream(dil) for dil in dils] * 2 + [row(d), _const_spec(w.shape)],
        out_specs=row(d),
        out_shape=jax.ShapeDtypeStruct((t, d), F32),
        scratch_shapes=[buf, buf],
        compiler_params=_params(1),
        name=name,
    )(*outs, *lses, x, w)


def _ffn_chunks(h, x, prev_fn, wg_ref, wu_ref, cw_ref, cb_ref, wd_ref, o_ref, act_scr, st_fn):
    for c in range(D_FF // FFN_CHUNK):
        sl = slice(c * FFN_CHUNK, (c + 1) * FFN_CHUNK)
        gt = _dot(h, wg_ref[:, sl])
        up = _dot(h, wu_ref[:, sl])
        p1, p2 = prev_fn(gt, sl)
        gc = cb_ref[:, sl] + cw_ref[0:1, sl] * p2
        gc = gc + cw_ref[1:2, sl] * p1
        gc = gc + cw_ref[2:3, sl] * gt
        act_scr[:, sl] = ((gc * jax.nn.sigmoid(gc)) * up).astype(BF16)
        st_fn(gt, sl)
    o_ref[...] = x + _dot(act_scr[...], wd_ref[...])


def _ffn_prompt_kernel(tiles_per_seq, x_ref, xh_ref, g_ref, wg_ref, wu_ref, cw_ref, cb_ref, wd_ref,
                       o_ref, st_ref, act_scr):
    x = x_ref[...]
    tm = x.shape[0]
    g = g_ref[...]
    h = _rms_rows(x, g).astype(BF16)
    hh = _rms_rows(xh_ref[...], g).astype(BF16)
    keep = (pl.program_id(0) % tiles_per_seq != 0).astype(F32)

    def prev_fn(gt, sl):
        gh = _dot(hh, wg_ref[:, sl]) * keep
        full = jnp.concatenate([gh, gt], axis=0)
        return pltpu.roll(full, 1, 0)[8:], pltpu.roll(full, 2, 0)[8:]

    def st_fn(gt, sl):
        st_ref[0, :, sl] = gt[tm - 8:]

    _ffn_chunks(h, x, prev_fn, wg_ref, wu_ref, cw_ref, cb_ref, wd_ref, o_ref, act_scr, st_fn)


def _ffn_sample_kernel(seq, x_ref, e1_ref, e2_ref, g_ref, wg_ref, wu_ref, cw_ref, cb_ref, wd_ref,
                       o_ref, st_ref, act_scr):
    x = x_ref[...]
    tm = x.shape[0]
    h = _rms_rows(x, g_ref[...]).astype(BF16)
    tpos = lax.broadcasted_iota(jnp.int32, (tm, 1), 0) & (seq - 1)

    def prev_fn(gt, sl):
        p1 = jnp.where(tpos >= 1, pltpu.roll(gt, 1, 0), e1_ref[:, sl])
        p2 = jnp.where(tpos >= 2, pltpu.roll(gt, 2, 0), e2_ref[:, sl])
        return p1, p2

    def st_fn(gt, sl):
        st_ref[:, sl] = gt

    _ffn_chunks(h, x, prev_fn, wg_ref, wu_ref, cw_ref, cb_ref, wd_ref, o_ref, act_scr, st_fn)


def _ffn_weight_specs(wg, wu, cw, cb, wd):
    return [_const_spec(wg.shape), _const_spec(wu.shape), _const_spec(cw.shape), _const_spec(cb.shape),
            _const_spec(wd.shape)]


def _ffn_prompt_call(name, x, seq_len, g, wg, wu, cw, cb, wd, tm):
    t, d = x.shape
    tps = seq_len // tm
    hb = tm // 8
    return pl.pallas_call(
        functools.partial(_ffn_prompt_kernel, tps),
        grid=(t // tm,),
        in_specs=[pl.BlockSpec((tm, d), lambda i: (i, 0)),
                  pl.BlockSpec((8, d), lambda i: (jnp.maximum(i * hb - 1, 0), 0)),
                  _const_spec((1, d))] + _ffn_weight_specs(wg, wu, cw, cb, wd),
        out_specs=[pl.BlockSpec((tm, d), lambda i: (i, 0)),
                   pl.BlockSpec((1, 8, D_FF), lambda i: (i // tps, 0, 0))],
        out_shape=[jax.ShapeDtypeStruct((t, d), F32),
                   jax.ShapeDtypeStruct((t // seq_len, 8, D_FF), F32)],
        scratch_shapes=[pltpu.VMEM((tm, D_FF), BF16)],
        compiler_params=_params(1),
        name=name,
    )(x, x, g, wg, wu, cw, cb, wd)


def _ffn_sample_call(name, x, seq, e1, e2, g, wg, wu, cw, cb, wd, tm):
    t, d = x.shape
    row = lambda n: pl.BlockSpec((tm, n), lambda i: (i, 0))
    return pl.pallas_call(
        functools.partial(_ffn_sample_kernel, seq),
        grid=(t // tm,),
        in_specs=[row(d), row(D_FF), row(D_FF), _const_spec((1, d))] + _ffn_weight_specs(wg, wu, cw, cb, wd),
        out_specs=[row(d), row(D_FF)],
        out_shape=[jax.ShapeDtypeStruct((t, d), F32), jax.ShapeDtypeStruct((t, D_FF), F32)],
        scratch_shapes=[pltpu.VMEM((tm, D_FF), BF16)],
        compiler_params=_params(1),
        name=name,
    )(x, e1, e2, g, wg, wu, cw, cb, wd)


def _a_prompt_kernel(nsub, q_ref, kp_ref, kc_ref, vp_ref, vc_ref, o_ref, lse_ref):
    step = pl.program_id(2)
    q_all = q_ref[0, 0].astype(BF16)
    k_all = jnp.concatenate([kp_ref[0, 0], kc_ref[0, 0]], axis=0).astype(BF16)
    v_all = jnp.concatenate([vp_ref[0, 0], vc_ref[0, 0]], axis=0).astype(BF16)
    qi = lax.broadcasted_iota(jnp.int32, (A_BLOCK, 2 * A_BLOCK), 0) + A_BLOCK
    ki = lax.broadcasted_iota(jnp.int32, (A_BLOCK, 2 * A_BLOCK), 1)
    dlt = qi - ki
    band = (dlt >= 0) & (dlt <= A_BLOCK)
    first_key = jnp.where(step > 0, 0, A_BLOCK)
    lo_lane = lax.broadcasted_iota(jnp.int32, (1, 2 * A_HEAD_DIM), 1) < A_HEAD_DIM
    for sb in range(nsub):
        rows = slice(sb * A_BLOCK, (sb + 1) * A_BLOCK)
        keys = slice(sb * A_BLOCK, (sb + 2) * A_BLOCK)
        mask = (band & (ki >= first_key)) if sb == 0 else band
        for p in range(A_HEADS // 2):
            sl = slice(p * 2 * A_HEAD_DIM, (p + 1) * 2 * A_HEAD_DIM)
            qp, kpair, vpair = q_all[rows, sl], k_all[keys, sl], v_all[keys, sl]
            o_pair = jnp.zeros((A_BLOCK, 2 * A_HEAD_DIM), F32)
            lse_pair = jnp.zeros((A_BLOCK, 2 * A_HEAD_DIM), F32)
            for half in range(2):
                hm = lo_lane if half == 0 else jnp.logical_not(lo_lane)
                s = _dot_nt(jnp.where(hm, qp, jnp.zeros_like(qp)), kpair)
                s = jnp.where(mask, s, NEG_INF)
                m = jnp.max(s, axis=-1, keepdims=True)
                e = jnp.exp(s - m)
                l = jnp.sum(e, axis=-1, keepdims=True)
                pv = _dot(e.astype(BF16), jnp.where(hm, vpair, jnp.zeros_like(vpair)))
                o_pair = o_pair + pv / l
                lse_pair = jnp.where(hm, m + jnp.log(l), lse_pair)
            o_ref[0, 0, rows, sl] = o_pair
            lse_ref[0, 0, rows, sl] = lse_pair


def _a_prompt_call(name, stream):
    batch, dil, m_len, _ = stream.shape
    nb = m_len // A_BLOCK
    nsub = min(A_BLOCKS_PER_STEP, nb)
    blk = (1, 1, nsub * A_BLOCK, A_GW)
    cur = lambda off: pl.BlockSpec(blk, lambda b, r, n: (b, r, n, off))
    prev = lambda off: pl.BlockSpec((1, 1, A_BLOCK, A_GW),
                                    lambda b, r, n: (b, r, jnp.maximum(n * nsub - 1, 0), off))
    o_spec = pl.BlockSpec(blk, lambda b, r, n: (b, r, n, 0))
    return pl.pallas_call(
        functools.partial(_a_prompt_kernel, nsub),
        grid=(batch, dil, nb // nsub),
        in_specs=[cur(0), prev(1), cur(1), prev(2), cur(2)],
        out_specs=[o_spec, o_spec],
        out_shape=[jax.ShapeDtypeStruct((batch, dil, m_len, A_GW), F32)] * 2,
        compiler_params=_params(3),
        name=name,
    )(stream, stream, stream, stream, stream)


def _a_sample_kernel(seq, qkv_ref, c0_ref, c1_ref, c2_ref, o_ref):
    rows = A_HEADS * seq
    qkv = qkv_ref[0]
    blockmask = _own_block_mask((rows, A_GW), seq, A_HEAD_DIM)
    outs, lses = [], []
    for g, ((window, dil), c_ref) in enumerate(zip(A_GROUPS, (c0_ref, c1_ref, c2_ref))):
        base = g * 3 * A_GW
        q = qkv[:, base:base + A_GW]
        kn = qkv[:, base + A_GW:base + 2 * A_GW]
        vn = qkv[:, base + 2 * A_GW:base + 3 * A_GW]
        qblk = jnp.where(blockmask, jnp.concatenate([q] * A_HEADS, axis=0), 0.0).astype(BF16)
        buf_len = c_ref.shape[4]
        kct = c_ref[0, 0, 0].astype(BF16)
        vct = c_ref[0, 0, 1].astype(BF16)
        s = _dot(qblk, kct)
        kidx = lax.broadcasted_iota(jnp.int32, (rows, buf_len), 1)
        trow = lax.broadcasted_iota(jnp.int32, (rows, buf_len), 0) & (seq - 1)
        back = buf_len + trow - kidx
        valid = (back <= window) & ((back & (dil - 1)) == 0)
        s = jnp.where(valid, s, NEG_INF)
        zpad = jnp.zeros((A_BLOCK - seq, A_GW), F32)
        knp = jnp.concatenate([kn, zpad], axis=0).astype(BF16)
        vnp = jnp.concatenate([vn, zpad], axis=0).astype(BF16)
        s2 = _dot_nt(qblk, knp)
        u = lax.broadcasted_iota(jnp.int32, (rows, A_BLOCK), 1)
        t2 = lax.broadcasted_iota(jnp.int32, (rows, A_BLOCK), 0) & (seq - 1)
        valid2 = (u <= t2) & (((t2 - u) & (dil - 1)) == 0)
        s2 = jnp.where(valid2, s2, NEG_INF)
        m = jnp.maximum(jnp.max(s, axis=-1, keepdims=True), jnp.max(s2, axis=-1, keepdims=True))
        e = jnp.exp(s - m)
        e2 = jnp.exp(s2 - m)
        l = jnp.sum(e, axis=-1, keepdims=True) + jnp.sum(e2, axis=-1, keepdims=True)
        outs.append((_dot_nt(e.astype(BF16), vct) + _dot(e2.astype(BF16), vnp)) / l)
        lses.append(m + jnp.log(l))
    mx = functools.reduce(jnp.maximum, lses)
    es = [jnp.exp(l - mx) for l in lses]
    den = es[0] + es[1] + es[2]
    merged = (es[0] * outs[0] + es[1] * outs[1] + es[2] * outs[2]) / den
    merged = jnp.where(blockmask, merged, 0.0)
    o = merged[0:seq]
    for hh in range(1, A_HEADS):
        o = o + merged[hh * seq:(hh + 1) * seq]
    o_ref[0] = o


def _a_sample_call(name, qkv, caches, layer, dec_batch, seq):
    cols = qkv.shape[1]
    views = [jnp.transpose(c, (0, 1, 3, 4, 5, 2)).reshape(c.shape[0], c.shape[1], 2, A_GW, c.shape[2])
             for c in caches]
    cspec = lambda v: pl.BlockSpec((1, 1, 2, A_GW, v.shape[4]), lambda b: (layer, b, 0, 0, 0))
    return pl.pallas_call(
        functools.partial(_a_sample_kernel, seq),
        grid=(dec_batch,),
        in_specs=[pl.BlockSpec((1, seq, cols), lambda b: (b, 0, 0))] + [cspec(v) for v in views],
        out_specs=pl.BlockSpec((1, seq, A_GW), lambda b: (b, 0, 0)),
        out_shape=jax.ShapeDtypeStruct((dec_batch, seq, A_GW), F32),
        compiler_params=_params(1),
        name=name,
    )(qkv.reshape(dec_batch, seq, cols), *views)


def _flash_kernel(vheads, finalize, qt_ref, k_ref, vt_ref, *rest):
    extra = rest[:-4]
    o_ref, m_scr, l_scr, acc_scr = rest[-4:]
    i = pl.program_id(1)
    j = pl.program_id(2)
    tq = qt_ref.shape[2]
    tk = k_ref.shape[1]

    @pl.when(j == 0)
    def _():
        m_scr[...] = jnp.full(m_scr.shape, NEG_INF, F32)
        l_scr[...] = jnp.zeros(l_scr.shape, F32)
        acc_scr[...] = jnp.zeros(acc_scr.shape, F32)

    def step(masked):
        if masked:
            causal = (lax.broadcasted_iota(jnp.int32, (tk, tq), 0)
                      <= lax.broadcasted_iota(jnp.int32, (tk, tq), 1))
        for vh, (qs, ks, vs) in enumerate(vheads):
            st = _dot(k_ref[0, :, ks], qt_ref[0, qs, :])
            if masked:
                st = jnp.where(causal, st, NEG_INF)
            m_prev = m_scr[vh]
            m_new = jnp.maximum(m_prev, jnp.max(st, axis=0, keepdims=True))
            alpha = jnp.exp2(m_prev - m_new)
            p = jnp.exp2(st - m_new)
            l_scr[vh] = alpha * l_scr[vh] + jnp.sum(p, axis=0, keepdims=True)
            acc_scr[vh] = alpha * acc_scr[vh] + _dot(vt_ref[0, vs, :], p.astype(BF16))
            m_scr[vh] = m_new

    @pl.when(j < i)
    def _():
        step(False)

    @pl.when(j == i)
    def _():
        step(True)
        finalize(extra, o_ref, l_scr, acc_scr)


def _b_finalize(lam_init, extra, o_ref, l_scr, acc_scr):
    lam_ref, subln_ref = extra
    lam = lam_ref[...]
    for hh in range(B_HEADS):
        o1 = acc_scr[2 * hh] / l_scr[2 * hh]
        o2 = acc_scr[2 * hh + 1] / l_scr[2 * hh + 1]
        od = o1 - lam * o2
        ms = jnp.mean(od * od, axis=0, keepdims=True)
        od = (od * lax.rsqrt(ms + NORM_EPS) * subln_ref[...]) * (1.0 - lam_init)
        o_ref[0, :, hh * B_PAIR:(hh + 1) * B_PAIR] = od.T.astype(o_ref.dtype)


def _c_finalize(extra, o_ref, l_scr, acc_scr):
    for hh in range(C_HEADS):
        o_ref[0, :, hh * C_V:(hh + 1) * C_V] = (acc_scr[hh] / l_scr[hh]).T.astype(o_ref.dtype)


def _flash_call(name, qt, k, vt, extra, vheads, finalize, dv, n_out, tile):
    batch, dq, seq = qt.shape
    tq = min(tile, seq)
    nq = seq // tq
    kv = k.reshape(batch, seq, k.shape[1])
    nvh = len(vheads)
    out = pl.pallas_call(
        functools.partial(_flash_kernel, tuple(vheads), finalize),
        grid=(batch, nq, nq),
        in_specs=[pl.BlockSpec((1, dq, tq), lambda b, i, j: (b, 0, i)),
                  pl.BlockSpec((1, tq, kv.shape[2]), lambda b, i, j: (b, jnp.minimum(i, j), 0)),
                  pl.BlockSpec((1, vt.shape[1], tq), lambda b, i, j: (b, 0, jnp.minimum(i, j)))]
                 + [_const_spec(e.shape) for e in extra],
        out_specs=pl.BlockSpec((1, tq, n_out), lambda b, i, j: (b, i, 0)),
        out_shape=jax.ShapeDtypeStruct((batch, seq, n_out), BF16),
        scratch_shapes=[pltpu.VMEM((nvh, 1, tq), F32), pltpu.VMEM((nvh, 1, tq), F32),
                        pltpu.VMEM((nvh, dv, tq), F32)],
        compiler_params=_params(3),
        name=name,
    )(qt, kv, vt, *extra)
    return out.reshape(batch * seq, n_out)


def _online_update(s, m_scr, l_scr):
    m_prev = m_scr[...]
    m_new = jnp.maximum(m_prev, jnp.max(s, axis=-1, keepdims=True))
    alpha = jnp.exp(m_prev - m_new)
    p = jnp.exp(s - m_new)
    l_scr[...] = alpha * l_scr[...] + jnp.sum(p, axis=-1, keepdims=True)
    m_scr[...] = m_new
    return alpha, p.astype(BF16)


def _new_row_mask(rows, seq):
    u = lax.broadcasted_iota(jnp.int32, (rows, PAGE_SIZE), 1)
    t = lax.broadcasted_iota(jnp.int32, (rows, PAGE_SIZE), 0) & (seq - 1)
    return u <= t


def _b_sample_kernel(n_pages, seq, lam_init, pt_ref, q_ref, kvn_ref, lam_ref, subln_ref, *rest):
    page_refs = rest[:n_pages]
    o_ref, qall_scr, kv_scr, m_scr, l_scr, acc_scr = rest[n_pages:]
    j = pl.program_id(1)
    rows = B_KV_HEADS * 2 * B_REP * seq
    half_rows = rows // B_KV_HEADS
    kw = B_KV_HEADS * B_PAIR
    nsec = 2 * B_KV_HEADS

    @pl.when(j == 0)
    def _():
        q = q_ref[0]
        lane = lax.broadcasted_iota(jnp.int32, (1, B_PAIR), 1)
        zero = jnp.zeros((seq, B_PAIR), F32)
        pieces = []
        for n in range(B_KV_HEADS):
            for c in range(2):
                keep = (lane < B_HEAD_DIM) if c == 0 else (lane >= B_HEAD_DIM)
                for g in range(B_REP):
                    hh = n * B_REP + g
                    qp = jnp.where(keep, q[:, hh * B_PAIR:(hh + 1) * B_PAIR], 0.0)
                    parts = [zero] * B_KV_HEADS
                    parts[n] = qp
                    pieces.append(jnp.concatenate(parts, axis=1))
        qall_scr[...] = jnp.concatenate(pieces, axis=0).astype(BF16)
        m_scr[...] = jnp.full(m_scr.shape, NEG_INF, F32)
        l_scr[...] = jnp.zeros(l_scr.shape, F32)
        acc_scr[...] = jnp.zeros(acc_scr.shape, F32)

    for p in range(n_pages):
        for sec in range(nsec):
            kv_scr[p * PAGE_SIZE:(p + 1) * PAGE_SIZE, sec * B_PAIR:(sec + 1) * B_PAIR] = (
                page_refs[p][0, pl.ds(sec, PAGE_SIZE, stride=nsec), :].astype(BF16))

    first_kv_head = lax.broadcasted_iota(jnp.int32, (rows, 1), 0) < half_rows

    def update(kt, vt, mask):
        s = _dot_nt(qall_scr[...], kt)
        if mask is not None:
            s = jnp.where(mask, s, NEG_INF)
        alpha, p = _online_update(s, m_scr, l_scr)
        r = _dot(p, vt)
        acc_scr[...] = alpha * acc_scr[...] + jnp.where(first_kv_head, r[:, :B_PAIR], r[:, B_PAIR:])

    update(kv_scr[:, :kw], kv_scr[:, kw:], None)

    @pl.when(j == pl.num_programs(1) - 1)
    def _():
        zpad = jnp.zeros((PAGE_SIZE - seq, 2 * kw), F32)
        new = jnp.concatenate([kvn_ref[0], zpad], axis=0).astype(BF16)
        update(new[:, :kw], new[:, kw:], _new_row_mask(rows, seq))
        o = acc_scr[...] / l_scr[...]
        lam = lam_ref[...]
        grp = B_REP * seq
        for n in range(B_KV_HEADS):
            o1 = o[n * 2 * grp:n * 2 * grp + grp]
            o2 = o[n * 2 * grp + grp:(n + 1) * 2 * grp]
            od = _rms_rows(o1 - lam * o2, subln_ref[...]) * (1.0 - lam_init)
            for g in range(B_REP):
                hh = n * B_REP + g
                o_ref[0, :, hh * B_PAIR:(hh + 1) * B_PAIR] = od[g * seq:(g + 1) * seq]


def _page_specs(n_pages, pages_per_seq, block):
    def spec(p):
        return pl.BlockSpec((1,) + block,
                            lambda b, j, pt: (pt[b * pages_per_seq + j * n_pages + p], 0, 0))
    return [spec(p) for p in range(n_pages)]


def _b_sample_call(name, q, kvn, lam, subln, cache, pt_flat, dec_batch, seq, pages_per_seq, lam_init):
    n_pages = min(PAGES_PER_STEP, pages_per_seq)
    width = 2 * B_KV_HEADS * B_PAIR
    rows = B_KV_HEADS * 2 * B_REP * seq
    per_seq = lambda n: pl.BlockSpec((1, seq, n), lambda b, j, pt: (b, 0, 0))
    grid_spec = pltpu.PrefetchScalarGridSpec(
        num_scalar_prefetch=1,
        grid=(dec_batch, pages_per_seq // n_pages),
        in_specs=[per_seq(q.shape[1]), per_seq(width),
                  pl.BlockSpec(lam.shape, lambda b, j, pt: (0, 0)),
                  pl.BlockSpec(subln.shape, lambda b, j, pt: (0, 0))]
                 + _page_specs(n_pages, pages_per_seq, cache.shape[1:]),
        out_specs=per_seq(B_HEADS * B_PAIR),
        scratch_shapes=[pltpu.VMEM((rows, B_KV_HEADS * B_PAIR), BF16),
                        pltpu.VMEM((n_pages * PAGE_SIZE, width), BF16),
                        pltpu.VMEM((rows, 1), F32), pltpu.VMEM((rows, 1), F32),
                        pltpu.VMEM((rows, B_PAIR), F32)])
    out = pl.pallas_call(
        functools.partial(_b_sample_kernel, n_pages, seq, lam_init),
        grid_spec=grid_spec,
        out_shape=jax.ShapeDtypeStruct((dec_batch, seq, B_HEADS * B_PAIR), F32),
        compiler_params=_params(2),
        name=name,
    )(pt_flat, q.reshape(dec_batch, seq, -1), kvn.reshape(dec_batch, seq, -1), lam, subln,
      *([cache] * n_pages))
    return out.reshape(dec_batch * seq, -1)


def _c_sample_kernel(n_pages, seq, pt_ref, q_ref, latn_ref, gk_ref, wukt_ref, wuk_ref, wuv_ref, *rest):
    page_refs = rest[:n_pages]
    o_ref, lhs_scr, qr_scr, lat_scr, m_scr, l_scr, acc_scr = rest[n_pages:]
    j = pl.program_id(1)
    rows = C_HEADS * seq
    nkx = C_HEADS * C_NOPE

    @pl.when(j == 0)
    def _():
        q = q_ref[0]
        zero = jnp.zeros((seq, C_NOPE), F32)
        blocks, ropes = [], []
        for hh in range(C_HEADS):
            parts = [zero] * C_HEADS
            parts[hh] = q[:, hh * C_QK_PAD:hh * C_QK_PAD + C_NOPE] * gk_ref[...]
            blocks.append(jnp.concatenate(parts, axis=1))
            ropes.append(q[:, hh * C_QK_PAD + C_NOPE:hh * C_QK_PAD + C_NOPE + C_ROPE])
        qn = jnp.concatenate(blocks, axis=0).astype(BF16)
        lhs_scr[:nkx] = wukt_ref[...]
        lhs_scr[nkx:] = _dot_nt(qn, wuk_ref[...]).astype(BF16)
        qr_scr[...] = jnp.concatenate(ropes, axis=0).astype(BF16)
        m_scr[...] = jnp.full(m_scr.shape, NEG_INF, F32)
        l_scr[...] = jnp.zeros(l_scr.shape, F32)
        acc_scr[...] = jnp.zeros(acc_scr.shape, F32)

    for p in range(n_pages):
        lat_scr[:, p * PAGE_SIZE:(p + 1) * PAGE_SIZE] = page_refs[p][0].astype(BF16)

    def scores(lt):
        ckvt = lt[:C_KV_LORA]
        kpet = lt[C_KV_LORA:]
        nk = lt.shape[1]
        scale_rows = []
        for hh in range(C_HEADS):
            kx = _dot(lhs_scr[hh * C_NOPE:(hh + 1) * C_NOPE, :], ckvt)
            ms = jnp.sum(kx * kx, axis=0, keepdims=True) * (1.0 / C_NOPE)
            scale_rows.append(jnp.broadcast_to(lax.rsqrt(ms + NORM_EPS), (seq, nk)))
        sq = _dot(lhs_scr[nkx:, :], ckvt)
        return sq * jnp.concatenate(scale_rows, axis=0) + _dot(qr_scr[...], kpet)

    def attend(s, ckvt):
        alpha, p = _online_update(s, m_scr, l_scr)
        acc_scr[...] = alpha * acc_scr[...] + _dot_nt(p, ckvt)

    sub = min(C_KEY_SUBTILE, n_pages * PAGE_SIZE)
    s_all = jnp.concatenate([scores(lat_scr[:, t * sub:(t + 1) * sub])
                             for t in range(n_pages * PAGE_SIZE // sub)], axis=1)
    attend(s_all, lat_scr[:C_KV_LORA, :])

    @pl.when(j == pl.num_programs(1) - 1)
    def _():
        new = latn_ref[0].astype(BF16)
        attend(jnp.where(_new_row_mask(rows, seq), scores(new), NEG_INF), new[:C_KV_LORA])
        o_lat = (acc_scr[...] / l_scr[...]).astype(BF16)
        r = _dot(o_lat, wuv_ref[...])
        r = jnp.where(_own_block_mask((rows, C_HEADS * C_V), seq, C_V), r, 0.0)
        o = r[0:seq]
        for hh in range(1, C_HEADS):
            o = o + r[hh * seq:(hh + 1) * seq]
        o_ref[0] = o


def _c_sample_call(name, q, latn_t, gk, wukt, wuk, wuv, cache, pt_flat, dec_batch, seq, pages_per_seq):
    n_pages = min(PAGES_PER_STEP, pages_per_seq)
    rows = C_HEADS * seq
    per_seq = lambda n: pl.BlockSpec((1, seq, n), lambda b, j, pt: (b, 0, 0))
    const = lambda a: pl.BlockSpec(a.shape, lambda b, j, pt: (0, 0))
    grid_spec = pltpu.PrefetchScalarGridSpec(
        num_scalar_prefetch=1,
        grid=(dec_batch, pages_per_seq // n_pages),
        in_specs=[per_seq(q.shape[1]),
                  pl.BlockSpec((1, C_LAT, PAGE_SIZE), lambda b, j, pt: (b, 0, 0)),
                  const(gk), const(wukt), const(wuk), const(wuv)]
                 + _page_specs(n_pages, pages_per_seq, cache.shape[1:]),
        out_specs=per_seq(C_HEADS * C_V),
        scratch_shapes=[pltpu.VMEM((C_HEADS * C_NOPE + rows, C_KV_LORA), BF16), pltpu.VMEM((rows, C_ROPE), BF16),
                        pltpu.VMEM((C_LAT, n_pages * PAGE_SIZE), BF16),
                        pltpu.VMEM((rows, 1), F32), pltpu.VMEM((rows, 1), F32),
                        pltpu.VMEM((rows, C_KV_LORA), F32)])
    out = pl.pallas_call(
        functools.partial(_c_sample_kernel, n_pages, seq),
        grid_spec=grid_spec,
        out_shape=jax.ShapeDtypeStruct((dec_batch, seq, C_HEADS * C_V), F32),
        compiler_params=_params(2),
        name=name,
    )(pt_flat, q.reshape(dec_batch, seq, -1), latn_t, gk, wukt, wuk, wuv, *([cache] * n_pages))
    return out.reshape(dec_batch * seq, -1)


def _block_mean_matrix(widths):
    m = np.zeros((LANE_CHUNK, LANE_CHUNK), np.float32)
    o = 0
    for w in widths:
        m[o:o + w, o:o + w] = 1.0 / w
        o += w
    return jnp.asarray(m, BF16)


def _rope_tables(pos):
    inv = ROPE_THETA ** (-jnp.arange(ROPE_HALF, dtype=F32) / ROPE_HALF)
    ang = pos.astype(F32)[:, None] * inv[None, :]
    cos, sin = jnp.cos(ang), jnp.sin(ang)
    return jnp.concatenate([cos, cos], axis=1), jnp.concatenate([-sin, sin], axis=1)


def _tables(pos):
    cos64, sin64 = _rope_tables(pos)
    n = pos.shape[0]
    one, zero = jnp.ones((n, ROPE_DIM), F32), jnp.zeros((n, ROPE_DIM), F32)
    return dict(
        full=(jnp.tile(cos64, (1, 4)), jnp.tile(sin64, (1, 4))),
        cq=(jnp.concatenate([one, one, cos64, one], 1), jnp.concatenate([zero, zero, sin64, zero], 1)),
        ck=(jnp.concatenate([cos64, one], 1), jnp.concatenate([sin64, zero], 1)))


def kernel(x_prompt, x_sample, cache_a_g0, cache_a_g1, cache_a_g2, cache_b_kv, cache_c_latent,
           state_ffn_conv, page_table, norm_attn, norm_ffn, a_w_qkv, a_q_norm, a_k_norm, a_w_o,
           b_w_qkv, b_q_norm, b_k_norm, b_lambda, b_subln, b_w_o, c_w_down, c_q_lora_norm, c_w_uq,
           c_kv_norm, c_w_ukv, c_q_norm, c_k_norm, c_w_o, ffn_w_gate, ffn_w_up, ffn_conv_w, ffn_conv_b,
           ffn_w_down):
    batch, seq, d = x_prompt.shape
    dec_batch, dec_seq, _ = x_sample.shape
    depth = norm_attn.shape[0]
    pages_per_seq = page_table.shape[1]
    past = pages_per_seq * PAGE_SIZE
    n_pool = cache_b_kv.shape[1] if cache_b_kv.shape[0] else cache_c_latent.shape[1]
    a_caches = (cache_a_g0, cache_a_g1, cache_a_g2)
    for (window, dil), c in zip(A_GROUPS, a_caches):
        assert c.shape[2] == window and seq % (dil * A_BLOCK) == 0
    assert dec_seq == 8 and A_BLOCK == PAGE_SIZE

    tp, ts = batch * seq, dec_batch * dec_seq
    tm_p, tm_s = min(TOKEN_TILE, seq), min(TOKEN_TILE, ts)
    xp = x_prompt.reshape(tp, d)
    xs = x_sample.reshape(ts, d)
    tab_p = _tables(jnp.arange(seq))
    tab_s = _tables(jnp.tile(past + jnp.arange(dec_seq), tm_s // dec_seq))
    pt_flat = page_table.reshape(-1)
    bmat64 = _block_mean_matrix((64, 64, 64, 64))
    bmat_cq = _block_mean_matrix((128, 64, 64))
    row = lambda v: v.reshape(1, -1).astype(F32)
    ng = len(A_GROUPS)
    dils = [dil for _, dil in A_GROUPS]

    a_p = [[] for _ in A_GROUPS]
    a_s = [[] for _ in A_GROUPS]
    b_p, b_s, c_p, c_s, f_p, f_s = [], [], [], [], [], []
    ia = ib = ic = 0
    for i in range(depth):
        g_attn = row(norm_attn[i])
        if i % N_MIXERS == 0:
            w = a_w_qkv[ia].astype(BF16)
            ncol = w.shape[1]
            gains = []
            for g in range(ng):
                gains += [jnp.tile(a_q_norm[ia, g], A_HEADS) * A_SCALE, jnp.tile(a_k_norm[ia, g], A_HEADS),
                          jnp.ones((A_GW,), F32)]
            gain = row(jnp.concatenate(gains))
            nch = ncol // LANE_CHUNK
            per_group = 3 * A_GW // LANE_CHUNK
            plan_p, plan_s = [], []
            for c in range(nch):
                g, cc = divmod(c, per_group)
                dests = [("stream", g, cc * LANE_CHUNK, dils[g])]
                if cc >= 2:
                    dests.append(("keep_t", ng + g, (cc - 2) // 2, cc % 2, min(A_GROUPS[g][0], seq)))
                plan_p.append((cc < 4, tuple(dests)))
                plan_s.append((cc < 4, (("rows", 0, c * LANE_CHUNK),)))
            outs_p = ([("stream", F32, 3 * A_GW, dil) for dil in dils]
                      + [("keep_t", F32, min(window, seq)) for window, _ in A_GROUPS])
            res = _proj_call(f"a{ia}_proj_p", xp, g_attn, w, bmat64, gain, *tab_p['full'], plan_p, outs_p,
                             batch, seq, tm_p)
            streams, keeps = res[:ng], res[ng:]
            qkv_s = _proj_call(f"a{ia}_proj_s", xs, g_attn, w, bmat64, gain, *tab_s['full'], plan_s,
                               [("rows", F32, ncol)], 1, ts, tm_s)[0]
            outs, lses = [], []
            for g in range(ng):
                o, l = _a_prompt_call(f"a{ia}_attn_p{g}", streams[g])
                outs.append(o)
                lses.append(l)
                keep = keeps[g].shape[3]
                a_p[g].append(jnp.transpose(keeps[g].reshape(batch, 2, A_HEADS, A_HEAD_DIM, keep), (0, 4, 1, 2, 3)))
                a_s[g].append(qkv_s.reshape(dec_batch, dec_seq, ng, 3, A_HEADS, A_HEAD_DIM)[:, :, g, 1:3])
            wo = a_w_o[ia].astype(BF16)
            xp = _a_out_call(f"a{ia}_out_p", outs, lses, dils, xp, wo, seq, tm_p)
            os_ = _a_sample_call(f"a{ia}_attn_s", qkv_s, a_caches, ia, dec_batch, dec_seq)
            xs = _out_call(f"a{ia}_out_s", os_.reshape(ts, A_GW), xs, wo, tm_s)
            ia += 1
        elif i % N_MIXERS == 1:
            lam_init = 0.8 - 0.6 * math.exp(-0.3 * i)
            lpf = b_lambda[ib].astype(F32)
            lam = (jnp.exp(jnp.sum(lpf[0] * lpf[1])) - jnp.exp(jnp.sum(lpf[2] * lpf[3])) + lam_init).reshape(1, 1)
            w = b_w_qkv[ib].astype(BF16)
            nq = B_HEADS * B_PAIR
            nk = B_KV_HEADS * B_PAIR
            gains = lambda q_scale: row(jnp.concatenate([
                jnp.tile(b_q_norm[ib], 2 * B_HEADS) * q_scale, jnp.tile(b_k_norm[ib], 2 * B_KV_HEADS),
                jnp.ones((nk,), F32)]))
            gain, gain_p = gains(B_SCALE), gains(B_SCALE * LOG2_E)
            nqc = nq // LANE_CHUNK
            assert nk == LANE_CHUNK
            plan_p = ([(True, (("cols_comp", 0, c * LANE_CHUNK),)) for c in range(nqc)]
                      + [(True, (("rows", 1, 0), ("rows", 2, 0))), (False, (("rows", 1, nk), ("cols", 3, 0)))])
            outs_p = [("cols", BF16, 2 * nq), ("rows", F32, 2 * nk), ("rows", BF16, nk), ("cols", BF16, nk)]
            qt_p, kv_p, kb_p, vt_p = _proj_call(f"b{ib}_proj_p", xp, g_attn, w, bmat64, gain_p, *tab_p['full'],
                                                plan_p, outs_p, batch, seq, tm_p)
            plan_s = ([(True, (("rows", 0, c * LANE_CHUNK),)) for c in range(nqc)]
                      + [(True, (("rows", 1, 0),)), (False, (("rows", 1, nk),))])
            q_s, kv_s = _proj_call(f"b{ib}_proj_s", xs, g_attn, w, bmat64, gain, *tab_s['full'], plan_s,
                                   [("rows", F32, nq), ("rows", F32, 2 * nk)], 1, ts, tm_s)
            b_p.append(kv_p.reshape(batch, seq, 2, B_KV_HEADS, B_PAIR))
            b_s.append(kv_s.reshape(dec_batch, dec_seq, 2, B_KV_HEADS, B_PAIR))
            vheads = [(slice(vh * B_PAIR, (vh + 1) * B_PAIR),
                       slice((vh // (2 * B_REP)) * B_PAIR, (vh // (2 * B_REP) + 1) * B_PAIR),
                       slice((vh // (2 * B_REP)) * B_PAIR, (vh // (2 * B_REP) + 1) * B_PAIR))
                      for vh in range(2 * B_HEADS)]
            o_p = _flash_call(f"b{ib}_attn_p", qt_p, kb_p, vt_p, [lam, b_subln[ib].reshape(-1, 1).astype(F32)],
                              vheads, functools.partial(_b_finalize, lam_init), B_PAIR, nq, B_FLASH_TILE)
            cache = cache_b_kv.reshape(cache_b_kv.shape[0] * n_pool, PAGE_SIZE * 2 * B_KV_HEADS, B_PAIR)
            o_s = _b_sample_call(f"b{ib}_attn_s", q_s, kv_s, lam, row(b_subln[ib]), cache, pt_flat + ib * n_pool,
                                 dec_batch, dec_seq, pages_per_seq, lam_init)
            wo = b_w_o[ib].astype(BF16)
            xp = _out_call(f"b{ib}_out_p", o_p, xp, wo, tm_p)
            xs = _out_call(f"b{ib}_out_s", o_s, xs, wo, tm_s)
            ib += 1
        else:
            wd = jnp.pad(c_w_down[ic], ((0, 0), (0, LANE - C_ROPE))).astype(BF16)
            wuq = c_w_uq[ic].reshape(C_Q_LORA, C_HEADS, C_NOPE + C_ROPE)
            wuq = jnp.pad(wuq, ((0, 0), (0, 0), (0, C_QK_PAD - C_NOPE - C_ROPE)))
            wuq = wuq.reshape(C_Q_LORA, C_HEADS * C_QK_PAD).astype(BF16)
            wukv = c_w_ukv[ic].reshape(C_KV_LORA, C_HEADS, C_NOPE + C_V)
            wuk = wukv[..., :C_NOPE].reshape(C_KV_LORA, C_HEADS * C_NOPE).astype(BF16)
            wuv = wukv[..., C_NOPE:].reshape(C_KV_LORA, C_HEADS * C_V).astype(BF16)
            qn, kn = c_q_norm[ic], c_k_norm[ic]
            gq = row(jnp.concatenate([qn, jnp.zeros((C_QK_PAD - C_NOPE - C_ROPE,), F32)]) * C_SCALE)
            gq_p = gq * LOG2_E
            kpg = row(jnp.concatenate([kn[C_NOPE:], jnp.zeros((LANE - C_ROPE,), F32)]))
            gk = row(kn[:C_NOPE])
            cproj = lambda nm, prompt, x, tab, nb, sq, tm: _c_proj_call(
                nm, prompt, x, g_attn, wd, row(c_q_lora_norm[ic]), row(c_kv_norm[ic]), kpg, tab['ck'][0],
                tab['ck'][1], wuq, bmat_cq, gq_p if prompt else gq, tab['cq'][0], tab['cq'][1], wuk, gk, wuv,
                nb, sq, tm)
            lat_p, qt_p, k_p, vt_p = cproj(f"c{ic}_proj_p", True, xp, tab_p, batch, seq, tm_p)
            lat_s, q_s = cproj(f"c{ic}_proj_s", False, xs, tab_s, 1, ts, tm_s)
            c_p.append(lat_p.reshape(batch, seq, C_LAT))
            c_s.append(lat_s.reshape(dec_batch, dec_seq, C_LAT))
            vheads = [(slice(hh * C_QK_PAD, (hh + 1) * C_QK_PAD), slice(hh * C_QK_PAD, (hh + 1) * C_QK_PAD),
                       slice(hh * C_V, (hh + 1) * C_V)) for hh in range(C_HEADS)]
            o_p = _flash_call(f"c{ic}_attn_p", qt_p, k_p, vt_p, [], vheads, _c_finalize, C_V, C_HEADS * C_V,
                              C_FLASH_TILE)
            cache = jnp.swapaxes(cache_c_latent, 2, 3).reshape(cache_c_latent.shape[0] * n_pool, C_LAT, PAGE_SIZE)
            latn_t = jnp.pad(jnp.swapaxes(lat_s.reshape(dec_batch, dec_seq, C_LAT), 1, 2),
                             ((0, 0), (0, 0), (0, PAGE_SIZE - dec_seq)))
            o_s = _c_sample_call(f"c{ic}_attn_s", q_s, latn_t, gk, wuk.T, wuk, wuv, cache,
                                 pt_flat + ic * n_pool, dec_batch, dec_seq, pages_per_seq)
            wo = c_w_o[ic].astype(BF16)
            xp = _out_call(f"c{ic}_out_p", o_p, xp, wo, tm_p)
            xs = _out_call(f"c{ic}_out_s", o_s, xs, wo, tm_s)
            ic += 1

        g_ffn = row(norm_ffn[i])
        wg, wu, wdn = ffn_w_gate[i].astype(BF16), ffn_w_up[i].astype(BF16), ffn_w_down[i].astype(BF16)
        cw, cb = ffn_conv_w[i].astype(F32), row(ffn_conv_b[i])
        xp, st_p = _ffn_prompt_call(f"ffn{i}_p", xp, seq, g_ffn, wg, wu, cw, cb, wdn, tm_p)
        ctx = state_ffn_conv[i]
        e1 = jnp.pad(ctx[:, 1:2], ((0, 0), (0, dec_seq - 1), (0, 0))).reshape(ts, D_FF)
        e2 = jnp.pad(ctx, ((0, 0), (0, dec_seq - 2), (0, 0))).reshape(ts, D_FF)
        xs, gate_s = _ffn_sample_call(f"ffn{i}_s", xs, dec_seq, e1, e2, g_ffn, wg, wu, cw, cb, wdn,
                                      min(FFN_SAMPLE_TILE, ts))
        f_p.append(st_p[:, 8 - (CONV_W - 1):])
        f_s.append(gate_s.reshape(dec_batch, dec_seq, D_FF)[:, dec_seq - (CONV_W - 1):])

    return (xp.reshape(batch, seq, d), xs.reshape(dec_batch, dec_seq, d),
            jnp.stack(a_p[0]), jnp.stack(a_p[1]), jnp.stack(a_p[2]),
            jnp.stack(b_p), jnp.stack(c_p), jnp.stack(f_p),
            jnp.stack(a_s[0]), jnp.stack(a_s[1]), jnp.stack(a_s[2]),
            jnp.stack(b_s), jnp.stack(c_s), jnp.stack(f_s))
```
